```python
import jax, jax.numpy as jnp
from jax import lax
import numpy as np

D_MODEL = 4096
BATCH = 4
SEQ = 2048
DEPTH = 2
DEC_BATCH = 8
DEC_SEQ = 1
PAST_LEN = 16384
PAGE_SIZE = 128

HEAD_DIM = 128
N_GROUPS = 3
HEADS_PER_GROUP = D_MODEL // (4 * HEAD_DIM)
ATTN_HEADS = N_GROUPS * HEADS_PER_GROUP
QKV_DIM = ATTN_HEADS * HEAD_DIM
ATTN_OUT = HEADS_PER_GROUP * HEAD_DIM
GROUPS = ((128, 1), (512, 4), (2048, 16))
BLOCK = 128
CONV_DIM = D_MODEL
CONV_WIDTH = 31
ROPE_THETA = 10000.0
EPS = 1e-6
NEG_INF = -1e30
IN_SIZES = (CONV_DIM, CONV_DIM, CONV_DIM, QKV_DIM, QKV_DIM, QKV_DIM, ATTN_OUT, D_MODEL, D_MODEL)
N_IN = 3 * CONV_DIM + 3 * QKV_DIM + ATTN_OUT + 2 * D_MODEL

kernel_name = 'dilated_conformer_gated_hybrid_step'


def _rmsnorm(x, g):
    x32 = x.astype(jnp.float32)
    y = x32 * lax.rsqrt(jnp.mean(x32 * x32, axis=-1, keepdims=True) + EPS)
    return (y * g.astype(jnp.float32)).astype(x.dtype)


def _layernorm(x, g, b):
    x32 = x.astype(jnp.float32)
    xc = x32 - jnp.mean(x32, axis=-1, keepdims=True)
    var = jnp.mean(xc * xc, axis=-1, keepdims=True)
    return (xc * lax.rsqrt(var + EPS) * g.astype(jnp.float32) + b.astype(jnp.float32)).astype(x.dtype)


def _rotary(t, pos):
    half = HEAD_DIM // 2
    inv_freq = ROPE_THETA ** (-jnp.arange(half, dtype=jnp.float32) / half)
    ang = pos.astype(jnp.float32)[:, None] * inv_freq[None, :]
    cos = jnp.cos(ang)[None, :, None, :]
    sin = jnp.sin(ang)[None, :, None, :]
    t32 = t.astype(jnp.float32)
    t1, t2 = t32[..., :half], t32[..., half:]
    return jnp.concatenate([t1 * cos - t2 * sin, t2 * cos + t1 * sin], axis=-1).astype(t.dtype)


def _softmax_lse(s, mask):
    s = jnp.where(mask, s, NEG_INF)
    m = jnp.max(s, axis=-1, keepdims=True)
    e = jnp.exp(s - m)
    den = jnp.sum(e, axis=-1, keepdims=True)
    return e / den, (m + jnp.log(den))[..., 0]


def _dilated_prompt(q, k, v, window, dil):
    B, S, H, E = q.shape
    span = dil * BLOCK
    s_pad = -(-S // span) * span
    n_sub = s_pad // dil
    nb = n_sub // BLOCK

    def to_sub(t):
        t = jnp.pad(t, ((0, 0), (0, s_pad - S), (0, 0), (0, 0)))
        t = t.reshape(B, n_sub, dil, H, E).transpose(0, 2, 1, 3, 4)
        return t.reshape(B, dil, nb, BLOCK, H, E)

    def with_prev(t):
        prev = jnp.pad(t[:, :, :-1], ((0, 0), (0, 0), (1, 0), (0, 0), (0, 0), (0, 0)))
        return jnp.concatenate([prev, t], axis=3)

    def from_sub(t):
        rest = t.shape[4:]
        t = t.reshape((B, dil, n_sub) + rest).swapaxes(1, 2).reshape((B, s_pad) + rest)
        return t[:, :S]

    qb = to_sub(q)
    kk = with_prev(to_sub(k))
    vv = with_prev(to_sub(v))
    s = jnp.einsum('brnqhe,brnkhe->brnhqk', qb, kk).astype(jnp.float32)
    i = jnp.arange(BLOCK)[:, None]
    m = jnp.arange(2 * BLOCK)[None, :]
    dist = BLOCK + i - m
    band = (dist >= 0) & (dist <= window // dil)
    blk = jnp.arange(nb)
    mask = band[None] & ((blk[:, None, None] > 0) | (m >= BLOCK)[None])
    p, lse = _softmax_lse(s, mask[None, None, :, None])
    o = jnp.einsum('brnhqk,brnkhe->brnqhe', p.astype(vv.dtype), vv)
    return from_sub(o), from_sub(jnp.swapaxes(lse, 3, 4))


def _dilated_sample(q, k_ext, v_ext, n_past, window, dil):
    T = q.shape[1]
    idx = n_past + jnp.arange(T)[:, None] - dil * jnp.arange(window // dil + 1)[None, :]
    valid = idx >= 0
    idx = jnp.maximum(idx, 0)
    kg = k_ext[:, idx]
    vg = v_ext[:, idx]
    s = jnp.einsum('bthe,btkhe->bthk', q, kg).astype(jnp.float32)
    p, lse = _softmax_lse(s, valid[None, :, None, :])
    o = jnp.einsum('bthk,btkhe->bthe', p.astype(vg.dtype), vg)
    return o, lse


def _depthwise_causal(u_ext, w):
    return lax.conv_general_dilated(u_ext, w[:, None, :], window_strides=(1,), padding='VALID',
                                    dimension_numbers=('NWC', 'WIO', 'NWC'),
                                    feature_group_count=CONV_DIM)


def _layer(x, pos, conv_ctx, kv_bufs, norm_g, w_in, dw_w, dw_b, ln_g, ln_b, w_pc, w_pa, w_o):
    B, T, _ = x.shape
    h = _rmsnorm(x, norm_g)
    z = h @ w_in
    cuts = np.cumsum(IN_SIZES)[:-1].tolist()
    glu_a, glu_b, c_gate, q, k, v, a_gate, g_conv, g_attn = jnp.split(z, cuts, axis=-1)

    u = glu_a * jax.nn.sigmoid(glu_b)
    u_ext = jnp.concatenate([conv_ctx.astype(u.dtype), u], axis=1)
    c = _depthwise_causal(u_ext, dw_w) + dw_b
    c = jax.nn.silu(_layernorm(c, ln_g, ln_b)) * jax.nn.silu(c_gate)
    p_conv = c @ w_pc
    new_conv = u_ext[:, -(CONV_WIDTH - 1):]

    q = _rotary(q.reshape(B, T, ATTN_HEADS, HEAD_DIM), pos) * (HEAD_DIM ** -0.5)
    k = _rotary(k.reshape(B, T, ATTN_HEADS, HEAD_DIM), pos)
    q = q.reshape(B, T, N_GROUPS, HEADS_PER_GROUP, HEAD_DIM)
    k = k.reshape(B, T, N_GROUPS, HEADS_PER_GROUP, HEAD_DIM)
    v = v.reshape(B, T, N_GROUPS, HEADS_PER_GROUP, HEAD_DIM)
    outs, lses, new_kv = [], [], []
    for g, (window, dil) in enumerate(GROUPS):
        qg, kg, vg = q[:, :, g], k[:, :, g], v[:, :, g]
        if kv_bufs is None:
            o, lse = _dilated_prompt(qg, kg, vg, window, dil)
            keep = min(window, T)
            new_kv.append(jnp.stack([kg, vg], axis=2)[:, T - keep:])
        else:
            buf = kv_bufs[g].astype(kg.dtype)
            n_past = buf.shape[1]
            k_ext = jnp.concatenate([buf[:, :, 0], kg], axis=1)
            v_ext = jnp.concatenate([buf[:, :, 1], vg], axis=1)
            o, lse = _dilated_sample(qg, k_ext, v_ext, n_past, window, dil)
            new_kv.append(jnp.stack([k_ext, v_ext], axis=2)[:, -n_past:])
        outs.append(o)
        lses.append(lse)
    wts = jax.nn.softmax(jnp.stack(lses, axis=0), axis=0)
    o = jnp.sum(wts[..., None] * jnp.stack(outs, axis=0).astype(jnp.float32), axis=0).astype(x.dtype)
    a = o.reshape(B, T, ATTN_OUT) * jax.nn.silu(a_gate)
    p_attn = a @ w_pa

    merged = jax.nn.sigmoid(g_conv) * p_conv + jax.nn.sigmoid(g_attn) * p_attn
    return x + merged @ w_o, new_conv, new_kv


def setup_inputs(seed: int = 0) -> dict:
    key = jax.random.key(seed)
    ks = jax.random.split(key, 18)
    nrm = jax.random.normal
    f32 = jnp.float32
    def kv_cache(k, window):
        return nrm(k, (DEPTH, DEC_BATCH, min(window, PAST_LEN), 2, HEADS_PER_GROUP, HEAD_DIM), f32)
    return {
        'x_prompt': nrm(ks[0], (BATCH, SEQ, D_MODEL), f32),
        'x_sample': nrm(ks[1], (DEC_BATCH, DEC_SEQ, D_MODEL), f32),
        'cache_kv_w128': kv_cache(ks[2], GROUPS[0][0]),
        'cache_kv_w512': kv_cache(ks[3], GROUPS[1][0]),
        'cache_kv_w2048': kv_cache(ks[4], GROUPS[2][0]),
        'state_conv': 0.5 * nrm(ks[5], (DEPTH, DEC_BATCH, CONV_WIDTH - 1, CONV_DIM), f32),
        'norm_g': 1.0 + 0.02 * nrm(ks[6], (DEPTH, D_MODEL), f32),
        'w_in': nrm(ks[7], (DEPTH, D_MODEL, N_IN), f32) * D_MODEL ** -0.5,
        'dw_w': nrm(ks[8], (DEPTH, CONV_WIDTH, CONV_DIM), f32) * CONV_WIDTH ** -0.5,
        'dw_b': 0.02 * nrm(ks[9], (DEPTH, CONV_DIM), f32),
        'ln_g': 1.0 + 0.02 * nrm(ks[10], (DEPTH, CONV_DIM), f32),
        'ln_b': 0.02 * nrm(ks[11], (DEPTH, CONV_DIM), f32),
        'w_pc': nrm(ks[12], (DEPTH, CONV_DIM, D_MODEL), f32) * CONV_DIM ** -0.5,
        'w_pa': nrm(ks[13], (DEPTH, ATTN_OUT, D_MODEL), f32) * ATTN_OUT ** -0.5,
        'w_o': nrm(ks[14], (DEPTH, D_MODEL, D_MODEL), f32) * D_MODEL ** -0.5,
        'final_g': 1.0 + 0.02 * nrm(ks[15], (D_MODEL,), f32),
    }


def reference(x_prompt, x_sample, cache_kv_w128, cache_kv_w512, cache_kv_w2048, state_conv,
              norm_g, w_in, dw_w, dw_b, ln_g, ln_b, w_pc, w_pa, w_o, final_g):
    pos_p = jnp.arange(x_prompt.shape[1], dtype=jnp.int32)
    pos_s = PAST_LEN + jnp.arange(x_sample.shape[1], dtype=jnp.int32)
    hp, hs = x_prompt, x_sample
    zero_ctx = jnp.zeros((hp.shape[0], CONV_WIDTH - 1, CONV_DIM), hp.dtype)
    conv_p, conv_s = [], []
    kv_p = [[] for _ in GROUPS]
    kv_s = [[] for _ in GROUPS]
    for l in range(DEPTH):
        params = (norm_g[l], w_in[l], dw_w[l], dw_b[l], ln_g[l], ln_b[l], w_pc[l], w_pa[l], w_o[l])
        hp, cp, kvp = _layer(hp, pos_p, zero_ctx, None, *params)
        hs, cs, kvs = _layer(hs, pos_s, state_conv[l],
                             (cache_kv_w128[l], cache_kv_w512[l], cache_kv_w2048[l]), *params)
        conv_p.append(cp)
        conv_s.append(cs)
        for g in range(N_GROUPS):
            kv_p[g].append(kvp[g])
            kv_s[g].append(kvs[g])
    y_prompt = _rmsnorm(hp, final_g)
    y_sample = _rmsnorm(hs, final_g)
    return (y_prompt, y_sample,
            jnp.stack(kv_p[0]), jnp.stack(kv_p[1]), jnp.stack(kv_p[2]), jnp.stack(conv_p),
            jnp.stack(kv_s[0]), jnp.stack(kv_s[1]), jnp.stack(kv_s[2]), jnp.stack(conv_s))
```

```python
import functools

import jax
import jax.numpy as jnp
from jax import lax
from jax.experimental import pallas as pl
from jax.experimental.pallas import tpu as pltpu

HEAD_DIM = 128
GROUPS = ((128, 1), (512, 4), (2048, 16))
N_GROUPS = len(GROUPS)
BLOCK = 128
CONV_WIDTH = 31
ROPE_THETA = 10000.0
EPS = 1e-6
NEG_INF = -1e30
PAST_LEN = 16384

_F32 = jnp.float32
_BF16 = jnp.bfloat16
_LANES = 128
_SUBLANES = 8
_HALO = 32
_VMEM_CAP = 60 * 2**20


def _params(semantics, vmem_bytes):
    return pltpu.CompilerParams(dimension_semantics=semantics,
                                vmem_limit_bytes=int(min(_VMEM_CAP, vmem_bytes)))


def _nbytes(shape, dtype):
    n = 1
    for s in shape:
        n *= s
    return n * jnp.dtype(dtype).itemsize


def _sigmoid(x):
    return 1.0 / (1.0 + jnp.exp(-x))


def _silu(x):
    return x * _sigmoid(x)


def _tile(n, pref):
    if n <= pref:
        return n
    t = pref
    while n % t:
        t //= 2
    return t


def _rms_kernel(x_ref, g_ref, o_ref):
    x = x_ref[...]
    ms = jnp.mean(x * x, axis=-1, keepdims=True)
    o_ref[...] = (x * lax.rsqrt(ms + EPS) * g_ref[...]).astype(o_ref.dtype)


def _rmsnorm(x, g, out_dtype, name):
    m, d = x.shape
    tr = _tile(m, 256)
    blk = _nbytes((tr, d), _F32)
    return pl.pallas_call(
        _rms_kernel,
        grid=(m // tr,),
        in_specs=[pl.BlockSpec((tr, d), lambda i: (i, 0)),
                  pl.BlockSpec((1, d), lambda i: (0, 0))],
        out_specs=pl.BlockSpec((tr, d), lambda i: (i, 0)),
        out_shape=jax.ShapeDtypeStruct((m, d), out_dtype),
        compiler_params=_params(("parallel",), 6 * blk + (8 << 20)),
        name=name,
    )(x, g.reshape(1, d))


def _dot(a, b):
    return jnp.dot(a.astype(_BF16), b.astype(_BF16), preferred_element_type=_F32)


def _w_spec(k, tn, layer, col_block):
    return pl.BlockSpec((None, k, tn), lambda i, j: (layer, 0, col_block + j))


def _mm_vmem(tm, tn, ks, n_io_tiles):
    total = 0
    for k in ks:
        total += 2 * _nbytes((tm, k), _BF16) + 2 * _nbytes((k, tn), _F32) + _nbytes((k, tn), _BF16)
    total += (2 * n_io_tiles + 2) * _nbytes((tm, tn), _F32)
    return total + (6 << 20)


def _glu_kernel(h_ref, wa_ref, wb_ref, o_ref):
    h = h_ref[...]
    o_ref[...] = _dot(h, wa_ref[...]) * _sigmoid(_dot(h, wb_ref[...]))


def _mm_glu(h, w, layer, col_a, col_b, n, name):
    m, k = h.shape
    tm, tn = _tile(m, 1024), _tile(n, 256)
    return pl.pallas_call(
        _glu_kernel,
        grid=(m // tm, n // tn),
        in_specs=[pl.BlockSpec((tm, k), lambda i, j: (i, 0)),
                  _w_spec(k, tn, layer, col_a // tn),
                  _w_spec(k, tn, layer, col_b // tn)],
        out_specs=pl.BlockSpec((tm, tn), lambda i, j: (i, j)),
        out_shape=jax.ShapeDtypeStruct((m, n), _F32),
        compiler_params=_params(("parallel", "arbitrary"), _mm_vmem(tm, tn, (k, k), 1) - 2 * _nbytes((tm, k), _BF16)),
        name=name,
    )(h, w, w)


def _act_kernel(h_ref, w_ref, o_ref, *, act):
    o_ref[...] = act(_dot(h_ref[...], w_ref[...]))


def _mm_act(h, w, layer, col, n, act, name):
    m, k = h.shape
    tm, tn = _tile(m, 1024), _tile(n, 512)
    return pl.pallas_call(
        functools.partial(_act_kernel, act=act),
        grid=(m // tm, n // tn),
        in_specs=[pl.BlockSpec((tm, k), lambda i, j: (i, 0)),
                  _w_spec(k, tn, layer, col // tn)],
        out_specs=pl.BlockSpec((tm, tn), lambda i, j: (i, j)),
        out_shape=jax.ShapeDtypeStruct((m, n), _F32),
        compiler_params=_params(("parallel", "arbitrary"), _mm_vmem(tm, tn, (k,), 1)),
        name=name,
    )(h, w)


def _rotate(z, cos, sin):
    parts = []
    for t in range(z.shape[1] // HEAD_DIM):
        x = z[:, t * HEAD_DIM:(t + 1) * HEAD_DIM]
        parts.append(x * cos + pltpu.roll(x, HEAD_DIM // 2, axis=1) * sin)
    return parts


def _rot_kernel(h_ref, w_ref, cos_ref, sin_ref, o_ref):
    z = _dot(h_ref[...], w_ref[...])
    for t, part in enumerate(_rotate(z, cos_ref[...], sin_ref[...])):
        o_ref[:, t * HEAD_DIM:(t + 1) * HEAD_DIM] = part


def _table_spec(tm, table_rows):
    tiles = table_rows // tm
    return pl.BlockSpec((tm, HEAD_DIM), lambda i, j: (i % tiles, 0))


def _mm_rot(h, w, layer, col, n, cos, sin, name):
    m, k = h.shape
    tm, tn = _tile(m, 1024), _tile(n, 512)
    return pl.pallas_call(
        _rot_kernel,
        grid=(m // tm, n // tn),
        in_specs=[pl.BlockSpec((tm, k), lambda i, j: (i, 0)),
                  _w_spec(k, tn, layer, col // tn),
                  _table_spec(tm, cos.shape[0]), _table_spec(tm, sin.shape[0])],
        out_specs=pl.BlockSpec((tm, tn), lambda i, j: (i, j)),
        out_shape=jax.ShapeDtypeStruct((m, n), _F32),
        compiler_params=_params(("parallel", "arbitrary"), _mm_vmem(tm, tn, (k,), 2)),
        name=name,
    )(h, w, cos, sin)


def _kv_kernel(h_ref, w_ref, cos_ref, sin_ref, o_ref, *, k_tiles):
    z = _dot(h_ref[...], w_ref[...])
    is_k = pl.program_id(1) < k_tiles

    @pl.when(is_k)
    def _():
        for t, part in enumerate(_rotate(z, cos_ref[...], sin_ref[...])):
            o_ref[:, t * HEAD_DIM:(t + 1) * HEAD_DIM] = part

    @pl.when(jnp.logical_not(is_k))
    def _():
        o_ref[...] = z


def _mm_kv(h, w, layer, col_k, col_v, n, cos, sin, name):
    m, k = h.shape
    tm, tn = _tile(m, 1024), _tile(n, 512)
    kt = n // tn
    kb, vb = col_k // tn, col_v // tn
    return pl.pallas_call(
        functools.partial(_kv_kernel, k_tiles=kt),
        grid=(m // tm, 2 * kt),
        in_specs=[pl.BlockSpec((tm, k), lambda i, j: (i, 0)),
                  pl.BlockSpec((None, k, tn), lambda i, j: (layer, 0, jnp.where(j < kt, kb + j, vb + j - kt))),
                  _table_spec(tm, cos.shape[0]), _table_spec(tm, sin.shape[0])],
        out_specs=pl.BlockSpec((tm, tn), lambda i, j: (i, j)),
        out_shape=jax.ShapeDtypeStruct((m, 2 * n), _F32),
        compiler_params=_params(("parallel", "arbitrary"), _mm_vmem(tm, tn, (k,), 2)),
        name=name,
    )(h, w, cos, sin)


def _merged_kernel(c_ref, a_ref, wc_ref, wa_ref, gc_ref, ga_ref, o_ref):
    pc = _dot(c_ref[...], wc_ref[...])
    pa = _dot(a_ref[...], wa_ref[...])
    o_ref[...] = (gc_ref[...] * pc + ga_ref[...] * pa).astype(o_ref.dtype)


def _mm_merged(c, a, w_pc, w_pa, gates, layer, name):
    m, kc = c.shape
    ka = a.shape[1]
    n = w_pc.shape[2]
    tm, tn = _tile(m, 1024), _tile(n, 256)
    out_dtype = _BF16 if m % 16 == 0 else _F32
    return pl.pallas_call(
        _merged_kernel,
        grid=(m // tm, n // tn),
        in_specs=[pl.BlockSpec((tm, kc), lambda i, j: (i, 0)),
                  pl.BlockSpec((tm, ka), lambda i, j: (i, 0)),
                  _w_spec(kc, tn, layer, 0), _w_spec(ka, tn, layer, 0),
                  pl.BlockSpec((tm, tn), lambda i, j: (i, j)),
                  pl.BlockSpec((tm, tn), lambda i, j: (i, j + n // tn))],
        out_specs=pl.BlockSpec((tm, tn), lambda i, j: (i, j)),
        out_shape=jax.ShapeDtypeStruct((m, n), out_dtype),
        compiler_params=_params(("parallel", "arbitrary"), _mm_vmem(tm, tn, (kc, ka), 3)),
        name=name,
    )(c, a, w_pc, w_pa, gates, gates)


def _out_kernel(m_ref, w_ref, x_ref, o_ref):
    o_ref[...] = x_ref[...] + _dot(m_ref[...], w_ref[...])


def _mm_out(merged, w_o, x, layer, name):
    m, k = merged.shape
    n = w_o.shape[2]
    tm, tn = _tile(m, 1024), _tile(n, 512)
    return pl.pallas_call(
        _out_kernel,
        grid=(m // tm, n // tn),
        in_specs=[pl.BlockSpec((tm, k), lambda i, j: (i, 0)),
                  _w_spec(k, tn, layer, 0),
                  pl.BlockSpec((tm, tn), lambda i, j: (i, j))],
        out_specs=pl.BlockSpec((tm, tn), lambda i, j: (i, j)),
        out_shape=jax.ShapeDtypeStruct((m, n), _F32),
        compiler_params=_params(("parallel", "arbitrary"), _mm_vmem(tm, tn, (k,), 2)),
        name=name,
    )(merged, w_o, x)


def _ln_gate(c, g, b, gate):
    mu = jnp.mean(c, axis=-1, keepdims=True)
    xc = c - mu
    var = jnp.mean(xc * xc, axis=-1, keepdims=True)
    return _silu(xc * lax.rsqrt(var + EPS) * g + b) * gate


def _conv_prompt_kernel(u_ref, halo_ref, gate_ref, w_ref, b_ref, g_ref, be_ref, o_ref, uext, cbuf,
                        *, tiles_per_seq, rows_conv, rows_ln):
    tr, ch = u_ref.shape
    first = (pl.program_id(0) % tiles_per_seq) == 0
    lead = _HALO - (CONV_WIDTH - 1)

    for ct in range(ch // _LANES):
        cs = slice(ct * _LANES, (ct + 1) * _LANES)
        uext[ct, 0:_HALO, :] = jnp.where(first, 0.0, halo_ref[:, cs])
        uext[ct, _HALO:_HALO + tr, :] = u_ref[:, cs]

    for ct in range(ch // _LANES):
        cs = slice(ct * _LANES, (ct + 1) * _LANES)
        taps = [w_ref[j:j + 1, cs] for j in range(CONV_WIDTH)]
        bias = b_ref[:, cs]

        def conv_rows(rg, carry, ct=ct, cs=cs, taps=taps, bias=bias):
            r0 = pl.multiple_of(rg * rows_conv, rows_conv)
            acc = jnp.broadcast_to(bias, (rows_conv, _LANES))
            for j in range(CONV_WIDTH):
                acc = acc + uext[ct, pl.ds(r0 + lead + j, rows_conv), :] * taps[j]
            cbuf[pl.ds(r0, rows_conv), cs] = acc
            return carry

        lax.fori_loop(0, tr // rows_conv, conv_rows, 0)

    def ln_rows(rg, carry):
        r0 = pl.multiple_of(rg * rows_ln, rows_ln)
        y = _ln_gate(cbuf[pl.ds(r0, rows_ln), :], g_ref[...], be_ref[...], gate_ref[pl.ds(r0, rows_ln), :])
        o_ref[pl.ds(r0, rows_ln), :] = y.astype(o_ref.dtype)
        return carry

    lax.fori_loop(0, tr // rows_ln, ln_rows, 0)


def _conv_prompt(u, gate, dw_w, dw_b, ln_g, ln_b, seq, name):
    m, ch = u.shape
    tr = _tile(seq, 256)
    row = lambda v: v.reshape(1, ch)
    blk = _nbytes((tr, ch), _F32)
    return pl.pallas_call(
        functools.partial(_conv_prompt_kernel, tiles_per_seq=seq // tr, rows_conv=64, rows_ln=32),
        grid=(m // tr,),
        in_specs=[pl.BlockSpec((tr, ch), lambda i: (i, 0)),
                  pl.BlockSpec((_HALO, ch), lambda i: (jnp.maximum(i * (tr // _HALO) - 1, 0), 0)),
                  pl.BlockSpec((tr, ch), lambda i: (i, 0)),
                  pl.BlockSpec((CONV_WIDTH, ch), lambda i: (0, 0)),
                  pl.BlockSpec((1, ch), lambda i: (0, 0)),
                  pl.BlockSpec((1, ch), lambda i: (0, 0)),
                  pl.BlockSpec((1, ch), lambda i: (0, 0))],
        out_specs=pl.BlockSpec((tr, ch), lambda i: (i, 0)),
        out_shape=jax.ShapeDtypeStruct((m, ch), _BF16),
        scratch_shapes=[pltpu.VMEM((ch // _LANES, _HALO + tr, _LANES), _F32), pltpu.VMEM((tr, ch), _F32)],
        compiler_params=_params(("parallel",), 8 * blk + (10 << 20)),
        name=name,
    )(u, u, gate, dw_w, row(dw_b), row(ln_g), row(ln_b))


def _conv_sample_kernel(uext_ref, gate_ref, w_ref, b_ref, g_ref, be_ref, o_ref, cbuf):
    for b in range(uext_ref.shape[0]):
        cbuf[b:b + 1, :] = jnp.sum(uext_ref[b] * w_ref[...], axis=0, keepdims=True) + b_ref[...]
    o_ref[...] = _ln_gate(cbuf[...], g_ref[...], be_ref[...], gate_ref[...])


def _conv_sample(uext, gate, dw_w, dw_b, ln_g, ln_b, name):
    nb, _, ch = uext.shape
    row = lambda v: v.reshape(1, ch)
    return pl.pallas_call(
        _conv_sample_kernel,
        out_shape=jax.ShapeDtypeStruct((nb, ch), _F32),
        scratch_shapes=[pltpu.VMEM((nb, ch), _F32)],
        compiler_params=_params((), 4 * _nbytes((nb, 32, ch), _F32) + (8 << 20)),
        name=name,
    )(uext, gate, dw_w, row(dw_b), row(ln_g), row(ln_b))


def _attn_prompt_kernel(q0, k0, v0, q1, k1, v1, q2, k2, v2, gate_ref, o_ref, obuf, lbuf):
    seq = o_ref.shape[0]
    ri = lax.broadcasted_iota(jnp.int32, (BLOCK, BLOCK), 0)
    ci = lax.broadcasted_iota(jnp.int32, (BLOCK, BLOCK), 1)
    keep_cur = ci <= ri
    keep_prev = ci >= ri
    nt = (((1,), (1,)), ((), ()))

    for g, ((window, dil), (q_ref, k_ref, v_ref)) in enumerate(
            zip(GROUPS, ((q0, k0, v0), (q1, k1, v1), (q2, k2, v2)))):
        assert window // dil == BLOCK and seq % (dil * BLOCK) == 0
        nb = seq // (dil * BLOCK)

        def rows(start, dil=dil):
            return pl.ds(start, BLOCK) if dil == 1 else pl.ds(start, BLOCK, stride=dil)

        def block(it, carry, g=g, dil=dil, nb=nb, q_ref=q_ref, k_ref=k_ref, v_ref=v_ref, rows=rows):
            res, n = it // nb, it % nb
            start = n * (BLOCK * dil) + res
            q = q_ref[rows(start), :].astype(_BF16)
            kc = k_ref[rows(start), :].astype(_BF16)
            vc = v_ref[rows(start), :].astype(_BF16)
            sc = jnp.where(keep_cur, lax.dot_general(q, kc, nt, preferred_element_type=_F32), NEG_INF)
            mx = jnp.max(sc, axis=-1, keepdims=True)
            if nb > 1:
                pstart = jnp.maximum(n - 1, 0) * (BLOCK * dil) + res
                kp = k_ref[rows(pstart), :].astype(_BF16)
                vp = v_ref[rows(pstart), :].astype(_BF16)
                sp = lax.dot_general(q, kp, nt, preferred_element_type=_F32)
                sp = jnp.where(jnp.logical_and(keep_prev, n > 0), sp, NEG_INF)
                mx = jnp.maximum(mx, jnp.max(sp, axis=-1, keepdims=True))
                ep = jnp.exp(sp - mx)
            ec = jnp.exp(sc - mx)
            den = jnp.sum(ec, axis=-1, keepdims=True)
            if nb > 1:
                den = den + jnp.sum(ep, axis=-1, keepdims=True)
            inv = 1.0 / den
            o = jnp.dot((ec * inv).astype(_BF16), vc, preferred_element_type=_F32)
            if nb > 1:
                o = o + jnp.dot((ep * inv).astype(_BF16), vp, preferred_element_type=_F32)
            obuf[g, rows(start), :] = o
            lbuf[g, rows(start), :] = jnp.broadcast_to(mx + jnp.log(den), (BLOCK, HEAD_DIM))
            return carry

        lax.fori_loop(0, seq // BLOCK, block, 0)

    rows_c = 256

    def combine(it, carry):
        r0 = pl.multiple_of(it * rows_c, rows_c)
        sl = pl.ds(r0, rows_c)
        l0, l1, l2 = lbuf[0, sl, :], lbuf[1, sl, :], lbuf[2, sl, :]
        mx = jnp.maximum(jnp.maximum(l0, l1), l2)
        e0, e1, e2 = jnp.exp(l0 - mx), jnp.exp(l1 - mx), jnp.exp(l2 - mx)
        inv = 1.0 / (e0 + e1 + e2)
        o = (e0 * inv) * obuf[0, sl, :] + (e1 * inv) * obuf[1, sl, :] + (e2 * inv) * obuf[2, sl, :]
        o_ref[sl, :] = (o * gate_ref[sl, :]).astype(o_ref.dtype)
        return carry

    lax.fori_loop(0, seq // rows_c, combine, 0)


def _attn_prompt(q, kvs, gate, batch, seq, name):
    hg = gate.shape[1] // HEAD_DIM
    q3 = q.reshape(batch, seq, q.shape[1])
    kv3 = [kv.reshape(batch, seq, kv.shape[1]) for kv in kvs]
    blk = (None, seq, HEAD_DIM)
    in_specs, args = [], []
    for g in range(N_GROUPS):
        in_specs += [pl.BlockSpec(blk, lambda b, h, g=g: (b, 0, g * hg + h)),
                     pl.BlockSpec(blk, lambda b, h: (b, 0, h)),
                     pl.BlockSpec(blk, lambda b, h: (b, 0, hg + h))]
        args += [q3, kv3[g], kv3[g]]
    in_specs.append(pl.BlockSpec(blk, lambda b, h: (b, 0, h)))
    args.append(gate.reshape(batch, seq, gate.shape[1]))
    tile = _nbytes((seq, HEAD_DIM), _F32)
    out = pl.pallas_call(
        _attn_prompt_kernel,
        grid=(batch, hg),
        in_specs=in_specs,
        out_specs=pl.BlockSpec(blk, lambda b, h: (b, 0, h)),
        out_shape=jax.ShapeDtypeStruct((batch, seq, hg * HEAD_DIM), _BF16),
        scratch_shapes=[pltpu.VMEM((N_GROUPS, seq, HEAD_DIM), _F32),
                        pltpu.VMEM((N_GROUPS, seq, HEAD_DIM), _F32)],
        compiler_params=_params(("parallel", "parallel"), 28 * tile + (8 << 20)),
        name=name,
    )(*args)
    return out.reshape(batch * seq, hg * HEAD_DIM)


def _attn_sample_kernel(q_ref, kv0, kv1, kv2, kb0, vb0, kb1, vb1, kb2, vb2, gate_ref, o_ref):
    outs, lses = [], []
    for g, (new_ref, kb_ref, vb_ref) in enumerate(((kv0, kb0, vb0), (kv1, kb1, vb1), (kv2, kb2, vb2))):
        q = q_ref[g]
        k_new, v_new = new_ref[0], new_ref[1]
        s_buf = jnp.sum(kb_ref[...] * q[None], axis=-1, keepdims=True)
        s_new = jnp.sum(k_new * q, axis=-1, keepdims=True)
        mx = jnp.maximum(jnp.max(s_buf, axis=0), s_new)
        e_buf = jnp.exp(s_buf - mx[None])
        e_new = jnp.exp(s_new - mx)
        den = jnp.sum(e_buf, axis=0) + e_new
        inv = 1.0 / den
        outs.append(jnp.sum((e_buf * inv[None]) * vb_ref[...], axis=0) + (e_new * inv) * v_new)
        lses.append(mx + jnp.log(den))
    mx = jnp.maximum(jnp.maximum(lses[0], lses[1]), lses[2])
    es = [jnp.exp(l - mx) for l in lses]
    inv = 1.0 / (es[0] + es[1] + es[2])
    o = (es[0] * inv) * outs[0] + (es[1] * inv) * outs[1] + (es[2] * inv) * outs[2]
    o_ref[...] = o * gate_ref[...]


def _attn_sample(q, kv_new, caches, gate, layer, name):
    nb = q.shape[0]
    hg = gate.shape[1] // HEAD_DIM
    in_specs = [pl.BlockSpec((None, N_GROUPS, hg, HEAD_DIM), lambda b: (b, 0, 0, 0))]
    args = [q.reshape(nb, N_GROUPS, hg, HEAD_DIM)]
    for g in range(N_GROUPS):
        in_specs.append(pl.BlockSpec((None, 2, hg, HEAD_DIM), lambda b: (b, 0, 0, 0)))
        args.append(kv_new[g].reshape(nb, 2, hg, HEAD_DIM))
    for g, (window, dil) in enumerate(GROUPS):
        cache = caches[g]
        assert cache.shape[2] == window, "window buffer shorter than the window is not supported"
        view = cache.reshape(cache.shape[0], nb, window // dil, dil, 2, hg, HEAD_DIM)
        for kv in range(2):
            in_specs.append(pl.BlockSpec((None, None, window // dil, None, None, hg, HEAD_DIM),
                                         lambda b, kv=kv: (layer, b, 0, 0, kv, 0, 0)))
            args.append(view)
    in_specs.append(pl.BlockSpec((None, hg, HEAD_DIM), lambda b: (b, 0, 0)))
    args.append(gate.reshape(nb, hg, HEAD_DIM))
    out = pl.pallas_call(
        _attn_sample_kernel,
        grid=(nb,),
        in_specs=in_specs,
        out_specs=pl.BlockSpec((None, hg, HEAD_DIM), lambda b: (b, 0, 0)),
        out_shape=jax.ShapeDtypeStruct((nb, hg, HEAD_DIM), _F32),
        compiler_params=_params(("parallel",), 24 * _nbytes((BLOCK, hg, HEAD_DIM), _F32) + (8 << 20)),
        name=name,
    )(*args)
    return out.reshape(nb, hg * HEAD_DIM)


def _rope_tables(pos, rows):
    half = HEAD_DIM // 2
    inv_freq = ROPE_THETA ** (-jnp.arange(half, dtype=_F32) / half)
    ang = pos.astype(_F32)[:, None] * inv_freq[None, :]
    cos, sin = jnp.cos(ang), jnp.sin(ang)
    cos = jnp.broadcast_to(jnp.concatenate([cos, cos], axis=-1), (rows, HEAD_DIM))
    sin = jnp.broadcast_to(jnp.concatenate([-sin, sin], axis=-1), (rows, HEAD_DIM))
    return cos, sin


def _projections(x, tables, layer, norm_g, w_in, tag):
    m, d = x.shape
    hg = d // (4 * HEAD_DIM)
    qkv, ao = N_GROUPS * hg * HEAD_DIM, hg * HEAD_DIM
    col = {}
    off = 0
    for nm, size in (("a", d), ("b", d), ("cg", d), ("q", qkv), ("k", qkv), ("v", qkv), ("ag", ao), ("g", 2 * d)):
        col[nm] = off
        off += size
    assert off == w_in.shape[2]
    cos, sin, cos_q, sin_q = tables
    h = _rmsnorm(x, norm_g[layer], _BF16 if m % 16 == 0 else _F32, f"rms_{tag}")
    u = _mm_glu(h, w_in, layer, col["a"], col["b"], d, f"glu_{tag}")
    cgate = _mm_act(h, w_in, layer, col["cg"], d, _silu, f"cgate_{tag}")
    q = _mm_rot(h, w_in, layer, col["q"], qkv, cos_q, sin_q, f"q_{tag}")
    kvs = [_mm_kv(h, w_in, layer, col["k"] + g * ao, col["v"] + g * ao, ao, cos, sin, f"kv{g}_{tag}")
           for g in range(N_GROUPS)]
    agate = _mm_act(h, w_in, layer, col["ag"], ao, _silu, f"agate_{tag}")
    gates = _mm_act(h, w_in, layer, col["g"], 2 * d, _sigmoid, f"gates_{tag}")
    return u, cgate, q, kvs, agate, gates


def kernel(x_prompt, x_sample, cache_kv_w128, cache_kv_w512, cache_kv_w2048, state_conv, norm_g, w_in,
           dw_w, dw_b, ln_g, ln_b, w_pc, w_pa, w_o, final_g):
    batch, seq, d = x_prompt.shape
    nb_s, seq_s, _ = x_sample.shape
    assert seq_s == 1, "the sample group is one new position per row"
    depth = w_in.shape[0]
    hg = d // (4 * HEAD_DIM)
    caches = (cache_kv_w128, cache_kv_w512, cache_kv_w2048)
    scale = HEAD_DIM ** -0.5

    cos_p, sin_p = _rope_tables(jnp.arange(seq, dtype=jnp.int32), seq)
    cos_s, sin_s = _rope_tables(jnp.full((1,), PAST_LEN, jnp.int32), nb_s)
    tab_p = (cos_p, sin_p, cos_p * scale, sin_p * scale)
    tab_s = (cos_s, sin_s, cos_s * scale, sin_s * scale)

    xp = x_prompt.reshape(batch * seq, d)
    xs = x_sample.reshape(nb_s, d)
    conv_p, conv_s = [], []
    kv_p = [[] for _ in GROUPS]
    kv_s = [[] for _ in GROUPS]
    for l in range(depth):
        u, cgate, q, kvs, agate, gates = _projections(xp, tab_p, l, norm_g, w_in, f"p{l}")
        c = _conv_prompt(u, cgate, dw_w[l], dw_b[l], ln_g[l], ln_b[l], seq, f"conv_p{l}")
        a = _attn_prompt(q, kvs, agate, batch, seq, f"attn_p{l}")
        merged = _mm_merged(c, a, w_pc, w_pa, gates, l, f"merged_p{l}")
        xp = _mm_out(merged, w_o, xp, l, f"out_p{l}")
        conv_p.append(u.reshape(batch, seq, d)[:, seq - (CONV_WIDTH - 1):])
        for g, (window, _) in enumerate(GROUPS):
            keep = min(window, seq)
            kv_p[g].append(kvs[g].reshape(batch, seq, 2, hg, HEAD_DIM)[:, seq - keep:])

        u, cgate, q, kvs, agate, gates = _projections(xs, tab_s, l, norm_g, w_in, f"s{l}")
        uext = jnp.concatenate([state_conv[l], u[:, None, :]], axis=1)
        c = _conv_sample(uext, cgate, dw_w[l], dw_b[l], ln_g[l], ln_b[l], f"conv_s{l}")
        a = _attn_sample(q, kvs, caches, agate, l, f"attn_s{l}")
        merged = _mm_merged(c, a, w_pc, w_pa, gates, l, f"merged_s{l}")
        xs = _mm_out(merged, w_o, xs, l, f"out_s{l}")
        conv_s.append(uext[:, 1:])
        for g in range(N_GROUPS):
            new = kvs[g].reshape(nb_s, 1, 2, hg, HEAD_DIM)
            kv_s[g].append(jnp.concatenate([caches[g][l][:, 1:], new], axis=1))

    y_prompt = _rmsnorm(xp, final_g, _F32, "final_p").reshape(batch, seq, d)
    y_sample = _rmsnorm(xs, final_g, _F32, "final_s").reshape(nb_s, 1, d)
    return (y_prompt, y_sample,
            jnp.stack(kv_p[0]), jnp.stack(kv_p[1]), jnp.stack(kv_p[2]), jnp.stack(conv_p),
            jnp.stack(kv_s[0]), jnp.stack(kv_s[1]), jnp.stack(kv_s[2]), jnp.stack(conv_s))
```

```python
import functools

import jax
import jax.numpy as jnp
from jax import lax
from jax.experimental import pallas as pl
from jax.experimental.pallas import tpu as pltpu

HEAD_DIM = 128
GROUPS = ((128, 1), (512, 4), (2048, 16))
N_GROUPS = len(GROUPS)
BLOCK = 128
CONV_WIDTH = 31
ROPE_THETA = 10000.0
EPS = 1e-6
NEG_INF = -1e30
PAST_LEN = 16384

_F32 = jnp.float32
_BF16 = jnp.bfloat16
_LANES = 128
_SUBLANES = 8
_HALO = 32
_ATTN_UNROLL = 8
_VMEM_CAP = 60 * 2**20


def _params(semantics, vmem_bytes):
    return pltpu.CompilerParams(dimension_semantics=semantics,
                                vmem_limit_bytes=int(min(_VMEM_CAP, vmem_bytes)))


def _nbytes(shape, dtype):
    n = 1
    for s in shape:
        n *= s
    return n * jnp.dtype(dtype).itemsize


def _sigmoid(x):
    return 1.0 / (1.0 + jnp.exp(-x))


def _silu(x):
    return x * _sigmoid(x)


def _tile(n, pref):
    if n <= pref:
        return n
    t = pref
    while n % t:
        t //= 2
    return t


def _rms_kernel(x_ref, g_ref, o_ref):
    x = x_ref[...]
    ms = jnp.mean(x * x, axis=-1, keepdims=True)
    o_ref[...] = (x * lax.rsqrt(ms + EPS) * g_ref[...]).astype(o_ref.dtype)


def _rmsnorm(x, g, out_dtype, name):
    m, d = x.shape
    tr = _tile(m, 256)
    blk = _nbytes((tr, d), _F32)
    return pl.pallas_call(
        _rms_kernel,
        grid=(m // tr,),
        in_specs=[pl.BlockSpec((tr, d), lambda i: (i, 0)),
                  pl.BlockSpec((1, d), lambda i: (0, 0))],
        out_specs=pl.BlockSpec((tr, d), lambda i: (i, 0)),
        out_shape=jax.ShapeDtypeStruct((m, d), out_dtype),
        compiler_params=_params(("parallel",), 6 * blk + (8 << 20)),
        name=name,
    )(x, g.reshape(1, d))


def _dot(a, b):
    return jnp.dot(a.astype(_BF16), b.astype(_BF16), preferred_element_type=_F32)


def _w_spec(k, tn, layer, col_block):
    return pl.BlockSpec((None, k, tn), lambda i, j: (layer, 0, col_block + j))


def _mm_vmem(tm, tn, ks, n_io_tiles):
    total = 0
    for k in ks:
        total += 2 * _nbytes((tm, k), _BF16) + 2 * _nbytes((k, tn), _F32) + _nbytes((k, tn), _BF16)
    total += (2 * n_io_tiles + 2) * _nbytes((tm, tn), _F32)
    return total + (6 << 20)


def _glu_kernel(h_ref, wa_ref, wb_ref, o_ref):
    h = h_ref[...]
    o_ref[...] = _dot(h, wa_ref[...]) * _sigmoid(_dot(h, wb_ref[...]))


def _mm_glu(h, w, layer, col_a, col_b, n, name):
    m, k = h.shape
    tm, tn = _tile(m, 1024), _tile(n, 256)
    return pl.pallas_call(
        _glu_kernel,
        grid=(m // tm, n // tn),
        in_specs=[pl.BlockSpec((tm, k), lambda i, j: (i, 0)),
                  _w_spec(k, tn, layer, col_a // tn),
                  _w_spec(k, tn, layer, col_b // tn)],
        out_specs=pl.BlockSpec((tm, tn), lambda i, j: (i, j)),
        out_shape=jax.ShapeDtypeStruct((m, n), _F32),
        compiler_params=_params(("parallel", "arbitrary"), _mm_vmem(tm, tn, (k, k), 1) - 2 * _nbytes((tm, k), _BF16)),
        name=name,
    )(h, w, w)


def _act_kernel(h_ref, w_ref, o_ref, *, act):
    o_ref[...] = act(_dot(h_ref[...], w_ref[...]))


def _mm_act(h, w, layer, col, n, act, name):
    m, k = h.shape
    tm, tn = _tile(m, 1024), _tile(n, 512)
    return pl.pallas_call(
        functools.partial(_act_kernel, act=act),
        grid=(m // tm, n // tn),
        in_specs=[pl.BlockSpec((tm, k), lambda i, j: (i, 0)),
                  _w_spec(k, tn, layer, col // tn)],
        out_specs=pl.BlockSpec((tm, tn), lambda i, j: (i, j)),
        out_shape=jax.ShapeDtypeStruct((m, n), _F32),
        compiler_params=_params(("parallel", "arbitrary"), _mm_vmem(tm, tn, (k,), 1)),
        name=name,
    )(h, w)


def _rotate(z, cos, sin):
    parts = []
    for t in range(z.shape[1] // HEAD_DIM):
        x = z[:, t * HEAD_DIM:(t + 1) * HEAD_DIM]
        parts.append(x * cos + pltpu.roll(x, HEAD_DIM // 2, axis=1) * sin)
    return parts


def _rot_kernel(h_ref, w_ref, cos_ref, sin_ref, o_ref):
    z = _dot(h_ref[...], w_ref[...])
    for t, part in enumerate(_rotate(z, cos_ref[...], sin_ref[...])):
        o_ref[:, t * HEAD_DIM:(t + 1) * HEAD_DIM] = part


def _table_spec(tm, table_rows):
    tiles = table_rows // tm
    return pl.BlockSpec((tm, HEAD_DIM), lambda i, j: (i % tiles, 0))


def _mm_rot(h, w, layer, col, n, cos, sin, name):
    m, k = h.shape
    tm, tn = _tile(m, 1024), _tile(n, 512)
    return pl.pallas_call(
        _rot_kernel,
        grid=(m // tm, n // tn),
        in_specs=[pl.BlockSpec((tm, k), lambda i, j: (i, 0)),
                  _w_spec(k, tn, layer, col // tn),
                  _table_spec(tm, cos.shape[0]), _table_spec(tm, sin.shape[0])],
        out_specs=pl.BlockSpec((tm, tn), lambda i, j: (i, j)),
        out_shape=jax.ShapeDtypeStruct((m, n), _F32),
        compiler_params=_params(("parallel", "arbitrary"), _mm_vmem(tm, tn, (k,), 2)),
        name=name,
    )(h, w, cos, sin)


def _kv_kernel(*refs, k_tiles, heads, stacked):
    h_ref, w_ref, cos_ref, sin_ref = refs[:4]
    o_ref = refs[-2] if stacked else refs[-1]
    st_ref = refs[-1] if stacked else None
    tm, tn = o_ref.shape
    z = _dot(h_ref[...], w_ref[...])
    j = pl.program_id(1)
    is_k = j < k_tiles
    per_tile = tn // HEAD_DIM

    if stacked and st_ref.shape[0] > 1:
        @pl.when(j == 0)
        def _():
            st_ref[1:] = jnp.zeros((st_ref.shape[0] - 1,) + st_ref.shape[1:], st_ref.dtype)

    def store(parts, first_head):
        for t, part in enumerate(parts):
            o_ref[:, t * HEAD_DIM:(t + 1) * HEAD_DIM] = part
            if stacked:
                st_ref[0, pl.ds(first_head + t, tm, stride=2 * heads), :] = part

    @pl.when(is_k)
    def _():
        store(_rotate(z, cos_ref[...], sin_ref[...]), j * per_tile)

    @pl.when(jnp.logical_not(is_k))
    def _():
        store([z[:, t * HEAD_DIM:(t + 1) * HEAD_DIM] for t in range(per_tile)], heads + (j - k_tiles) * per_tile)


def _mm_kv(h, w, layer, col_k, col_v, n, cos, sin, name, stack=None):
    m, k = h.shape
    heads = n // HEAD_DIM
    creates = bool(stack) and stack[1] is None
    tm, tn = _tile(m, 512 if creates else 1024), _tile(n, 256 if stack else 512)
    kt = n // tn
    kb, vb = col_k // tn, col_v // tn
    in_specs = [pl.BlockSpec((tm, k), lambda i, j: (i, 0)),
                pl.BlockSpec((None, k, tn), lambda i, j: (layer, 0, jnp.where(j < kt, kb + j, vb + j - kt))),
                _table_spec(tm, cos.shape[0]), _table_spec(tm, sin.shape[0])]
    args = [h, w, cos, sin]
    out_specs = [pl.BlockSpec((tm, tn), lambda i, j: (i, j))]
    out_shape = [jax.ShapeDtypeStruct((m, 2 * n), _F32)]
    aliases = {}
    vmem = _mm_vmem(tm, tn, (k,), 2)
    if stack:
        depth, previous = stack
        assert creates == (layer == 0)
        rows = tm * 2 * heads
        slots = depth if creates else 1
        out_specs.append(pl.BlockSpec((slots, rows, HEAD_DIM), lambda i, j: (layer, i, 0)))
        out_shape.append(jax.ShapeDtypeStruct((depth, m * 2 * heads, HEAD_DIM), _F32))
        vmem += 2 * _nbytes((slots, rows, HEAD_DIM), _F32)
        if previous is not None:
            in_specs.append(pl.BlockSpec(memory_space=pl.ANY))
            args.append(previous)
            aliases = {4: 1}
    outs = pl.pallas_call(
        functools.partial(_kv_kernel, k_tiles=kt, heads=heads, stacked=bool(stack)),
        grid=(m // tm, 2 * kt),
        in_specs=in_specs,
        out_specs=out_specs,
        out_shape=out_shape,
        input_output_aliases=aliases,
        compiler_params=_params(("parallel", "arbitrary"), vmem),
        name=name,
    )(*args)
    return tuple(outs) if stack else outs[0]


def _merged_kernel(c_ref, a_ref, wc_ref, wa_ref, gc_ref, ga_ref, o_ref):
    pc = _dot(c_ref[...], wc_ref[...])
    pa = _dot(a_ref[...], wa_ref[...])
    o_ref[...] = (gc_ref[...] * pc + ga_ref[...] * pa).astype(o_ref.dtype)


def _mm_merged(c, a, w_pc, w_pa, gates, layer, name):
    m, kc = c.shape
    ka = a.shape[1]
    n = w_pc.shape[2]
    tm, tn = _tile(m, 1024), _tile(n, 256)
    out_dtype = _BF16 if m % 16 == 0 else _F32
    return pl.pallas_call(
        _merged_kernel,
        grid=(m // tm, n // tn),
        in_specs=[pl.BlockSpec((tm, kc), lambda i, j: (i, 0)),
                  pl.BlockSpec((tm, ka), lambda i, j: (i, 0)),
                  _w_spec(kc, tn, layer, 0), _w_spec(ka, tn, layer, 0),
                  pl.BlockSpec((tm, tn), lambda i, j: (i, j)),
                  pl.BlockSpec((tm, tn), lambda i, j: (i, j + n // tn))],
        out_specs=pl.BlockSpec((tm, tn), lambda i, j: (i, j)),
        out_shape=jax.ShapeDtypeStruct((m, n), out_dtype),
        compiler_params=_params(("parallel", "arbitrary"), _mm_vmem(tm, tn, (kc, ka), 3)),
        name=name,
    )(c, a, w_pc, w_pa, gates, gates)


def _out_kernel(m_ref, w_ref, x_ref, o_ref):
    o_ref[...] = x_ref[...] + _dot(m_ref[...], w_ref[...])


def _mm_out(merged, w_o, x, layer, name):
    m, k = merged.shape
    n = w_o.shape[2]
    tm, tn = _tile(m, 1024), _tile(n, 512)
    return pl.pallas_call(
        _out_kernel,
        grid=(m // tm, n // tn),
        in_specs=[pl.BlockSpec((tm, k), lambda i, j: (i, 0)),
                  _w_spec(k, tn, layer, 0),
                  pl.BlockSpec((tm, tn), lambda i, j: (i, j))],
        out_specs=pl.BlockSpec((tm, tn), lambda i, j: (i, j)),
        out_shape=jax.ShapeDtypeStruct((m, n), _F32),
        compiler_params=_params(("parallel", "arbitrary"), _mm_vmem(tm, tn, (k,), 2)),
        name=name,
    )(merged, w_o, x)


def _ln_gate(c, g, b, gate):
    mu = jnp.mean(c, axis=-1, keepdims=True)
    xc = c - mu
    var = jnp.mean(xc * xc, axis=-1, keepdims=True)
    return _silu(xc * lax.rsqrt(var + EPS) * g + b) * gate


def _conv_prompt_kernel(u_ref, halo_ref, gate_ref, w_ref, b_ref, g_ref, be_ref, o_ref, uext, cbuf,
                        *, tiles_per_seq, rows_conv, rows_ln):
    tr, ch = u_ref.shape
    first = (pl.program_id(0) % tiles_per_seq) == 0
    lead = _HALO - (CONV_WIDTH - 1)

    for ct in range(ch // _LANES):
        cs = slice(ct * _LANES, (ct + 1) * _LANES)
        uext[ct, 0:_HALO, :] = jnp.where(first, 0.0, halo_ref[:, cs])
        uext[ct, _HALO:_HALO + tr, :] = u_ref[:, cs]

    for ct in range(ch // _LANES):
        cs = slice(ct * _LANES, (ct + 1) * _LANES)
        taps = [w_ref[j:j + 1, cs] for j in range(CONV_WIDTH)]
        bias = b_ref[:, cs]

        def conv_rows(rg, carry, ct=ct, cs=cs, taps=taps, bias=bias):
            r0 = pl.multiple_of(rg * rows_conv, rows_conv)
            acc = jnp.broadcast_to(bias, (rows_conv, _LANES))
            for j in range(CONV_WIDTH):
                acc = acc + uext[ct, pl.ds(r0 + lead + j, rows_conv), :] * taps[j]
            cbuf[pl.ds(r0, rows_conv), cs] = acc
            return carry

        lax.fori_loop(0, tr // rows_conv, conv_rows, 0)

    def ln_rows(rg, carry):
        r0 = pl.multiple_of(rg * rows_ln, rows_ln)
        y = _ln_gate(cbuf[pl.ds(r0, rows_ln), :], g_ref[...], be_ref[...], gate_ref[pl.ds(r0, rows_ln), :])
        o_ref[pl.ds(r0, rows_ln), :] = y.astype(o_ref.dtype)
        return carry

    lax.fori_loop(0, tr // rows_ln, ln_rows, 0)


def _conv_prompt(u, gate, dw_w, dw_b, ln_g, ln_b, seq, name):
    m, ch = u.shape
    tr = _tile(seq, 256)
    row = lambda v: v.reshape(1, ch)
    blk = _nbytes((tr, ch), _F32)
    return pl.pallas_call(
        functools.partial(_conv_prompt_kernel, tiles_per_seq=seq // tr, rows_conv=64, rows_ln=32),
        grid=(m // tr,),
        in_specs=[pl.BlockSpec((tr, ch), lambda i: (i, 0)),
                  pl.BlockSpec((_HALO, ch), lambda i: (jnp.maximum(i * (tr // _HALO) - 1, 0), 0)),
                  pl.BlockSpec((tr, ch), lambda i: (i, 0)),
                  pl.BlockSpec((CONV_WIDTH, ch), lambda i: (0, 0)),
                  pl.BlockSpec((1, ch), lambda i: (0, 0)),
                  pl.BlockSpec((1, ch), lambda i: (0, 0)),
                  pl.BlockSpec((1, ch), lambda i: (0, 0))],
        out_specs=pl.BlockSpec((tr, ch), lambda i: (i, 0)),
        out_shape=jax.ShapeDtypeStruct((m, ch), _BF16),
        scratch_shapes=[pltpu.VMEM((ch // _LANES, _HALO + tr, _LANES), _F32), pltpu.VMEM((tr, ch), _F32)],
        compiler_params=_params(("parallel",), 8 * blk + (10 << 20)),
        name=name,
    )(u, u, gate, dw_w, row(dw_b), row(ln_g), row(ln_b))


def _conv_sample_kernel(uext_ref, gate_ref, w_ref, b_ref, g_ref, be_ref, o_ref, cbuf):
    for b in range(uext_ref.shape[0]):
        cbuf[b:b + 1, :] = jnp.sum(uext_ref[b] * w_ref[...], axis=0, keepdims=True) + b_ref[...]
    o_ref[...] = _ln_gate(cbuf[...], g_ref[...], be_ref[...], gate_ref[...])


def _conv_sample(uext, gate, dw_w, dw_b, ln_g, ln_b, name):
    nb, _, ch = uext.shape
    row = lambda v: v.reshape(1, ch)
    return pl.pallas_call(
        _conv_sample_kernel,
        out_shape=jax.ShapeDtypeStruct((nb, ch), _F32),
        scratch_shapes=[pltpu.VMEM((nb, ch), _F32)],
        compiler_params=_params((), 4 * _nbytes((nb, 32, ch), _F32) + (8 << 20)),
        name=name,
    )(uext, gate, dw_w, row(dw_b), row(ln_g), row(ln_b))


def _attn_prompt_kernel(q0, k0, v0, q1, k1, v1, q2, k2, v2, gate_ref, o_ref, obuf, lbuf):
    seq = o_ref.shape[0]
    ri = lax.broadcasted_iota(jnp.int32, (BLOCK, BLOCK), 0)
    ci = lax.broadcasted_iota(jnp.int32, (BLOCK, BLOCK), 1)
    keep_cur = ci <= ri
    keep_prev = ci >= ri
    nt = (((1,), (1,)), ((), ()))

    for g, ((window, dil), (q_ref, k_ref, v_ref)) in enumerate(
            zip(GROUPS, ((q0, k0, v0), (q1, k1, v1), (q2, k2, v2)))):
        assert window // dil == BLOCK and seq % (dil * BLOCK) == 0
        nb = seq // (dil * BLOCK)

        def rows(start, dil=dil):
            return pl.ds(start, BLOCK) if dil == 1 else pl.ds(start, BLOCK, stride=dil)

        def blocks(it, carry, g=g, dil=dil, nb=nb, q_ref=q_ref, k_ref=k_ref, v_ref=v_ref, rows=rows):
            starts, pstarts, first, scores = [], [], [], []
            for k in range(_ATTN_UNROLL):
                blk = it * _ATTN_UNROLL + k
                res, n = blk // nb, blk % nb
                starts.append(n * (BLOCK * dil) + res)
                pstarts.append(jnp.maximum(n - 1, 0) * (BLOCK * dil) + res)
                first.append(n == 0)
            for k in range(_ATTN_UNROLL):
                q = q_ref[rows(starts[k]), :].astype(_BF16)
                kc = k_ref[rows(starts[k]), :].astype(_BF16)
                sc = jnp.where(keep_cur, lax.dot_general(q, kc, nt, preferred_element_type=_F32), NEG_INF)
                sp = None
                if nb > 1:
                    kp = k_ref[rows(pstarts[k]), :].astype(_BF16)
                    sp = lax.dot_general(q, kp, nt, preferred_element_type=_F32)
                    sp = jnp.where(jnp.logical_and(keep_prev, jnp.logical_not(first[k])), sp, NEG_INF)
                scores.append((sc, sp))
            probs = []
            for sc, sp in scores:
                if nb > 1:
                    mx = jnp.max(jnp.maximum(sc, sp), axis=-1, keepdims=True)
                    ec, ep = jnp.exp(sc - mx), jnp.exp(sp - mx)
                    den = jnp.sum(ec + ep, axis=-1, keepdims=True)
                    probs.append((ec.astype(_BF16), ep.astype(_BF16), mx, den))
                else:
                    mx = jnp.max(sc, axis=-1, keepdims=True)
                    ec = jnp.exp(sc - mx)
                    den = jnp.sum(ec, axis=-1, keepdims=True)
                    probs.append((ec.astype(_BF16), None, mx, den))
            for k, (ec, ep, mx, den) in enumerate(probs):
                o = jnp.dot(ec, v_ref[rows(starts[k]), :].astype(_BF16), preferred_element_type=_F32)
                if nb > 1:
                    o = o + jnp.dot(ep, v_ref[rows(pstarts[k]), :].astype(_BF16), preferred_element_type=_F32)
                obuf[g, rows(starts[k]), :] = o * (1.0 / den)
                lbuf[g, rows(starts[k]), :] = jnp.broadcast_to(mx + jnp.log(den), (BLOCK, HEAD_DIM))
            return carry

        lax.fori_loop(0, seq // (BLOCK * _ATTN_UNROLL), blocks, 0)

    rows_c = 256

    def combine(it, carry):
        r0 = pl.multiple_of(it * rows_c, rows_c)
        sl = pl.ds(r0, rows_c)
        l0, l1, l2 = lbuf[0, sl, :], lbuf[1, sl, :], lbuf[2, sl, :]
        mx = jnp.maximum(jnp.maximum(l0, l1), l2)
        e0, e1, e2 = jnp.exp(l0 - mx), jnp.exp(l1 - mx), jnp.exp(l2 - mx)
        inv = 1.0 / (e0 + e1 + e2)
        o = (e0 * inv) * obuf[0, sl, :] + (e1 * inv) * obuf[1, sl, :] + (e2 * inv) * obuf[2, sl, :]
        o_ref[sl, :] = (o * gate_ref[sl, :]).astype(o_ref.dtype)
        return carry

    lax.fori_loop(0, seq // rows_c, combine, 0)


def _attn_prompt(q, kvs, gate, batch, seq, name):
    hg = gate.shape[1] // HEAD_DIM
    q3 = q.reshape(batch, seq, q.shape[1])
    kv3 = [kv.reshape(batch, seq, kv.shape[1]) for kv in kvs]
    blk = (None, seq, HEAD_DIM)
    in_specs, args = [], []
    for g in range(N_GROUPS):
        in_specs += [pl.BlockSpec(blk, lambda b, h, g=g: (b, 0, g * hg + h)),
                     pl.BlockSpec(blk, lambda b, h: (b, 0, h)),
                     pl.BlockSpec(blk, lambda b, h: (b, 0, hg + h))]
        args += [q3, kv3[g], kv3[g]]
    in_specs.append(pl.BlockSpec(blk, lambda b, h: (b, 0, h)))
    args.append(gate.reshape(batch, seq, gate.shape[1]))
    tile = _nbytes((seq, HEAD_DIM), _F32)
    out = pl.pallas_call(
        _attn_prompt_kernel,
        grid=(batch, hg),
        in_specs=in_specs,
        out_specs=pl.BlockSpec(blk, lambda b, h: (b, 0, h)),
        out_shape=jax.ShapeDtypeStruct((batch, seq, hg * HEAD_DIM), _BF16),
        scratch_shapes=[pltpu.VMEM((N_GROUPS, seq, HEAD_DIM), _F32),
                        pltpu.VMEM((N_GROUPS, seq, HEAD_DIM), _F32)],
        compiler_params=_params(("parallel", "parallel"), 28 * tile + (8 << 20)),
        name=name,
    )(*args)
    return out.reshape(batch * seq, hg * HEAD_DIM)


def _attn_sample_kernel(q_ref, kv0, kv1, kv2, kb0, vb0, kb1, vb1, kb2, vb2, gate_ref, o_ref):
    outs, lses = [], []
    for g, (new_ref, kb_ref, vb_ref) in enumerate(((kv0, kb0, vb0), (kv1, kb1, vb1), (kv2, kb2, vb2))):
        q = q_ref[g]
        k_new, v_new = new_ref[0], new_ref[1]
        s_buf = jnp.sum(kb_ref[...] * q[None], axis=-1, keepdims=True)
        s_new = jnp.sum(k_new * q, axis=-1, keepdims=True)
        mx = jnp.maximum(jnp.max(s_buf, axis=0), s_new)
        e_buf = jnp.exp(s_buf - mx[None])
        e_new = jnp.exp(s_new - mx)
        den = jnp.sum(e_buf, axis=0) + e_new
        inv = 1.0 / den
        outs.append(jnp.sum((e_buf * inv[None]) * vb_ref[...], axis=0) + (e_new * inv) * v_new)
        lses.append(mx + jnp.log(den))
    mx = jnp.maximum(jnp.maximum(lses[0], lses[1]), lses[2])
    es = [jnp.exp(l - mx) for l in lses]
    inv = 1.0 / (es[0] + es[1] + es[2])
    o = (es[0] * inv) * outs[0] + (es[1] * inv) * outs[1] + (es[2] * inv) * outs[2]
    o_ref[...] = o * gate_ref[...]


def _attn_sample(q, kv_new, caches, gate, layer, name):
    nb = q.shape[0]
    hg = gate.shape[1] // HEAD_DIM
    in_specs = [pl.BlockSpec((None, N_GROUPS, hg, HEAD_DIM), lambda b: (b, 0, 0, 0))]
    args = [q.reshape(nb, N_GROUPS, hg, HEAD_DIM)]
    for g in range(N_GROUPS):
        in_specs.append(pl.BlockSpec((None, 2, hg, HEAD_DIM), lambda b: (b, 0, 0, 0)))
        args.append(kv_new[g].reshape(nb, 2, hg, HEAD_DIM))
    for g, (window, dil) in enumerate(GROUPS):
        cache = caches[g]
        assert cache.shape[2] == window, "window buffer shorter than the window is not supported"
        view = cache.reshape(cache.shape[0], nb, window // dil, dil, 2, hg, HEAD_DIM)
        for kv in range(2):
            in_specs.append(pl.BlockSpec((None, None, window // dil, None, None, hg, HEAD_DIM),
                                         lambda b, kv=kv: (layer, b, 0, 0, kv, 0, 0)))
            args.append(view)
    in_specs.append(pl.BlockSpec((None, hg, HEAD_DIM), lambda b: (b, 0, 0)))
    args.append(gate.reshape(nb, hg, HEAD_DIM))
    out = pl.pallas_call(
        _attn_sample_kernel,
        grid=(nb,),
        in_specs=in_specs,
        out_specs=pl.BlockSpec((None, hg, HEAD_DIM), lambda b: (b, 0, 0)),
        out_shape=jax.ShapeDtypeStruct((nb, hg, HEAD_DIM), _F32),
        compiler_params=_params(("parallel",), 24 * _nbytes((BLOCK, hg, HEAD_DIM), _F32) + (8 << 20)),
        name=name,
    )(*args)
    return out.reshape(nb, hg * HEAD_DIM)


def _cache_update_kernel(*refs):
    n = (len(refs) - 1) // 3
    sem = refs[-1]
    copies = []
    for g in range(n):
        cache, new, out = refs[2 * g], refs[2 * g + 1], refs[2 * n + g]
        keep = cache.shape[2] - 1
        copies.append(pltpu.make_async_copy(cache.at[:, :, pl.ds(1, keep)], out.at[:, :, pl.ds(0, keep)], sem.at[2 * g]))
        copies.append(pltpu.make_async_copy(new, out.at[:, :, keep], sem.at[2 * g + 1]))
    for c in copies:
        c.start()
    for c in copies:
        c.wait()


def _cache_update(caches, new_rows, name):
    args, in_specs = [], []
    for cache, new in zip(caches, new_rows):
        args += [cache, new]
        in_specs += [pl.BlockSpec(memory_space=pl.ANY), pl.BlockSpec(memory_space=pl.ANY)]
    return pl.pallas_call(
        _cache_update_kernel,
        in_specs=in_specs,
        out_specs=[pl.BlockSpec(memory_space=pl.ANY) for _ in caches],
        out_shape=[jax.ShapeDtypeStruct(c.shape, c.dtype) for c in caches],
        scratch_shapes=[pltpu.SemaphoreType.DMA((2 * len(caches),))],
        name=name,
    )(*args)


def _rope_tables(pos, rows):
    half = HEAD_DIM // 2
    inv_freq = ROPE_THETA ** (-jnp.arange(half, dtype=_F32) / half)
    ang = pos.astype(_F32)[:, None] * inv_freq[None, :]
    cos, sin = jnp.cos(ang), jnp.sin(ang)
    cos = jnp.broadcast_to(jnp.concatenate([cos, cos], axis=-1), (rows, HEAD_DIM))
    sin = jnp.broadcast_to(jnp.concatenate([-sin, sin], axis=-1), (rows, HEAD_DIM))
    return cos, sin


def _projections(x, tables, layer, norm_g, w_in, tag, stack_last=None):
    m, d = x.shape
    hg = d // (4 * HEAD_DIM)
    qkv, ao = N_GROUPS * hg * HEAD_DIM, hg * HEAD_DIM
    col = {}
    off = 0
    for nm, size in (("a", d), ("b", d), ("cg", d), ("q", qkv), ("k", qkv), ("v", qkv), ("ag", ao), ("g", 2 * d)):
        col[nm] = off
        off += size
    assert off == w_in.shape[2]
    cos, sin, cos_q, sin_q = tables
    h = _rmsnorm(x, norm_g[layer], _BF16 if m % 16 == 0 else _F32, f"rms_{tag}")
    u = _mm_glu(h, w_in, layer, col["a"], col["b"], d, f"glu_{tag}")
    cgate = _mm_act(h, w_in, layer, col["cg"], d, _silu, f"cgate_{tag}")
    q = _mm_rot(h, w_in, layer, col["q"], qkv, cos_q, sin_q, f"q_{tag}")
    kvs = [_mm_kv(h, w_in, layer, col["k"] + g * ao, col["v"] + g * ao, ao, cos, sin, f"kv{g}_{tag}",
                  stack=stack_last if g == N_GROUPS - 1 else None)
           for g in range(N_GROUPS)]
    agate = _mm_act(h, w_in, layer, col["ag"], ao, _silu, f"agate_{tag}")
    gates = _mm_act(h, w_in, layer, col["g"], 2 * d, _sigmoid, f"gates_{tag}")
    return u, cgate, q, kvs, agate, gates


def kernel(x_prompt, x_sample, cache_kv_w128, cache_kv_w512, cache_kv_w2048, state_conv, norm_g, w_in,
           dw_w, dw_b, ln_g, ln_b, w_pc, w_pa, w_o, final_g):
    batch, seq, d = x_prompt.shape
    nb_s, seq_s, _ = x_sample.shape
    assert seq_s == 1, "the sample group is one new position per row"
    depth = w_in.shape[0]
    hg = d // (4 * HEAD_DIM)
    caches = (cache_kv_w128, cache_kv_w512, cache_kv_w2048)
    scale = HEAD_DIM ** -0.5

    cos_p, sin_p = _rope_tables(jnp.arange(seq, dtype=jnp.int32), seq)
    cos_s, sin_s = _rope_tables(jnp.full((1,), PAST_LEN, jnp.int32), nb_s)
    tab_p = (cos_p, sin_p, cos_p * scale, sin_p * scale)
    tab_s = (cos_s, sin_s, cos_s * scale, sin_s * scale)

    xs = x_sample.reshape(nb_s, d)
    conv_s = []
    new_rows = [[] for _ in GROUPS]
    for l in range(depth):
        u, cgate, q, kvs, agate, gates = _projections(xs, tab_s, l, norm_g, w_in, f"s{l}")
        uext = jnp.concatenate([state_conv[l], u[:, None, :]], axis=1)
        c = _conv_sample(uext, cgate, dw_w[l], dw_b[l], ln_g[l], ln_b[l], f"conv_s{l}")
        a = _attn_sample(q, kvs, caches, agate, l, f"attn_s{l}")
        merged = _mm_merged(c, a, w_pc, w_pa, gates, l, f"merged_s{l}")
        xs = _mm_out(merged, w_o, xs, l, f"out_s{l}")
        conv_s.append(uext[:, 1:])
        for g in range(N_GROUPS):
            new_rows[g].append(kvs[g].reshape(nb_s, 2, hg, HEAD_DIM))
    kv_s = _cache_update(caches, [jnp.stack(rows) for rows in new_rows], "cache_update")

    xp = x_prompt.reshape(batch * seq, d)
    conv_p = []
    kv_p = [[] for _ in GROUPS]
    stack_last = GROUPS[-1][0] >= seq
    stacked = None
    for l in range(depth):
        u, cgate, q, kvs, agate, gates = _projections(xp, tab_p, l, norm_g, w_in, f"p{l}",
                                                      stack_last=(depth, stacked) if stack_last else None)
        if stack_last:
            kvs[-1], stacked = kvs[-1]
        c = _conv_prompt(u, cgate, dw_w[l], dw_b[l], ln_g[l], ln_b[l], seq, f"conv_p{l}")
        a = _attn_prompt(q, kvs, agate, batch, seq, f"attn_p{l}")
        merged = _mm_merged(c, a, w_pc, w_pa, gates, l, f"merged_p{l}")
        xp = _mm_out(merged, w_o, xp, l, f"out_p{l}")
        conv_p.append(u.reshape(batch, seq, d)[:, seq - (CONV_WIDTH - 1):])
        for g, (window, _) in enumerate(GROUPS):
            keep = min(window, seq)
            kv_p[g].append(kvs[g].reshape(batch, seq, 2, hg, HEAD_DIM)[:, seq - keep:])
    kv_p = [jnp.stack(rows) for rows in kv_p]
    if stack_last:
        kv_p[-1] = stacked.reshape(depth, batch, seq, 2, hg, HEAD_DIM)

    y_prompt = _rmsnorm(xp, final_g, _F32, "final_p").reshape(batch, seq, d)
    y_sample = _rmsnorm(xs, final_g, _F32, "final_s").reshape(nb_s, 1, d)
    return (y_prompt, y_sample, kv_p[0], kv_p[1], kv_p[2], jnp.stack(conv_p),
            kv_s[0], kv_s[1], kv_s[2], jnp.stack(conv_s))
```

```python
import functools

import jax
import jax.numpy as jnp
from jax import lax
from jax.experimental import pallas as pl
from jax.experimental.pallas import tpu as pltpu

HEAD_DIM = 128
GROUPS = ((128, 1), (512, 4), (2048, 16))
N_GROUPS = len(GROUPS)
BLOCK = 128
CONV_WIDTH = 31
ROPE_THETA = 10000.0
EPS = 1e-6
NEG_INF = -1e30
PAST_LEN = 16384

_F32 = jnp.float32
_BF16 = jnp.bfloat16
_LANES = 128
_SUBLANES = 8
_HALO = 32
_ATTN_UNROLL = 8
_VMEM_CAP = 60 * 2**20


def _params(semantics, vmem_bytes):
    return pltpu.CompilerParams(dimension_semantics=semantics,
                                vmem_limit_bytes=int(min(_VMEM_CAP, vmem_bytes)))


def _nbytes(shape, dtype):
    n = 1
    for s in shape:
        n *= s
    return n * jnp.dtype(dtype).itemsize


def _sigmoid(x):
    return 0.5 * jnp.tanh(0.5 * x) + 0.5


def _silu(x):
    return x * _sigmoid(x)


def _tile(n, pref):
    if n <= pref:
        return n
    t = pref
    while n % t:
        t //= 2
    return t


def _rms_kernel(x_ref, g_ref, o_ref):
    x = x_ref[...]
    ms = jnp.mean(x * x, axis=-1, keepdims=True)
    o_ref[...] = (x * lax.rsqrt(ms + EPS) * g_ref[...]).astype(o_ref.dtype)


def _rmsnorm(x, g, out_dtype, name):
    m, d = x.shape
    tr = _tile(m, 256)
    blk = _nbytes((tr, d), _F32)
    return pl.pallas_call(
        _rms_kernel,
        grid=(m // tr,),
        in_specs=[pl.BlockSpec((tr, d), lambda i: (i, 0)),
                  pl.BlockSpec((1, d), lambda i: (0, 0))],
        out_specs=pl.BlockSpec((tr, d), lambda i: (i, 0)),
        out_shape=jax.ShapeDtypeStruct((m, d), out_dtype),
        compiler_params=_params(("parallel",), 6 * blk + (8 << 20)),
        name=name,
    )(x, g.reshape(1, d))


def _dot(a, b):
    return jnp.dot(a.astype(_BF16), b.astype(_BF16), preferred_element_type=_F32)


def _w_spec(k, tn, layer, col_block):
    return pl.BlockSpec((None, k, tn), lambda i, j: (layer, 0, col_block + j))


def _mm_vmem(tm, tn, lhs_ks, w_ks, n_io_tiles, extra=0):
    total = sum(2 * _nbytes((tm, k), _BF16) for k in lhs_ks)
    total += sum(2 * _nbytes((k, tn), _F32) + _nbytes((k, tn), _BF16) for k in w_ks)
    total += (2 * n_io_tiles + len(w_ks) + 1) * _nbytes((tm, tn), _F32)
    return total + extra + (6 << 20)


def _glu_kernel(h_ref, wa_ref, wb_ref, o_ref):
    h = h_ref[...]
    o_ref[...] = _dot(h, wa_ref[...]) * _sigmoid(_dot(h, wb_ref[...]))


def _mm_glu(h, w, layer, col_a, col_b, n, name):
    m, k = h.shape
    tm, tn = _tile(m, 1024), _tile(n, 256)
    return pl.pallas_call(
        _glu_kernel,
        grid=(m // tm, n // tn),
        in_specs=[pl.BlockSpec((tm, k), lambda i, j: (i, 0)),
                  _w_spec(k, tn, layer, col_a // tn),
                  _w_spec(k, tn, layer, col_b // tn)],
        out_specs=pl.BlockSpec((tm, tn), lambda i, j: (i, j)),
        out_shape=jax.ShapeDtypeStruct((m, n), _F32),
        compiler_params=_params(("parallel", "arbitrary"), _mm_vmem(tm, tn, (k,), (k, k), 1)),
        name=name,
    )(h, w, w)


def _gates_kernel(h_ref, w_ref, o_ref, *, silu_tiles):
    z = _dot(h_ref[...], w_ref[...])
    s = _sigmoid(z)
    o_ref[...] = jnp.where(pl.program_id(1) < silu_tiles, z * s, s)


def _mm_gates(h, w, layer, col_c, n_c, col_rest, n_silu_rest, n_sig, name):
    m, k = h.shape
    n = n_c + n_silu_rest + n_sig
    tm, tn = _tile(m, 1024), _tile(n_silu_rest, 512)
    assert n_c % tn == 0 and n_sig % tn == 0 and col_c % tn == 0 and col_rest % tn == 0
    c_tiles, cb, rb = n_c // tn, col_c // tn, col_rest // tn
    return pl.pallas_call(
        functools.partial(_gates_kernel, silu_tiles=(n_c + n_silu_rest) // tn),
        grid=(m // tm, n // tn),
        in_specs=[pl.BlockSpec((tm, k), lambda i, j: (i, 0)),
                  pl.BlockSpec((None, k, tn), lambda i, j: (layer, 0, jnp.where(j < c_tiles, cb + j, rb + j - c_tiles)))],
        out_specs=pl.BlockSpec((tm, tn), lambda i, j: (i, j)),
        out_shape=jax.ShapeDtypeStruct((m, n), _F32),
        compiler_params=_params(("parallel", "arbitrary"), _mm_vmem(tm, tn, (k,), (k,), 1)),
        name=name,
    )(h, w)


def _rotate(x, cos, sin):
    return x * cos + pltpu.roll(x, HEAD_DIM // 2, axis=1) * sin


def _table_spec(tm, table_rows):
    tiles = table_rows // tm
    return pl.BlockSpec((tm, HEAD_DIM), lambda i, j: (i % tiles, 0))


def _qkv_kernel(h_ref, w_ref, cos_ref, sin_ref, o_ref, *, q_tiles, part_tiles, q_scale):
    z = _dot(h_ref[...], w_ref[...])
    j = pl.program_id(1)
    is_q = j < q_tiles
    is_v = jnp.logical_and(jnp.logical_not(is_q), ((j - q_tiles) % (2 * part_tiles)) >= part_tiles)
    scale = jnp.where(is_q, q_scale, 1.0).astype(_F32)
    cos, sin = cos_ref[...] * scale, sin_ref[...] * scale
    for t in range(z.shape[1] // HEAD_DIM):
        x = z[:, t * HEAD_DIM:(t + 1) * HEAD_DIM]
        o_ref[:, t * HEAD_DIM:(t + 1) * HEAD_DIM] = jnp.where(is_v, x, _rotate(x, cos, sin))


def _mm_qkv(h, w, layer, col_q, n_q, col_k, col_v, n_part, n_groups, cos, sin, name):
    m, k = h.shape
    tm, tn = _tile(m, 1024), _tile(n_part, 512)
    assert n_q % tn == 0 and col_q % tn == 0 and col_k % tn == 0 and col_v % tn == 0
    qt, pt = n_q // tn, n_part // tn
    qb, kb, vb = col_q // tn, col_k // tn, col_v // tn

    def w_block(i, j):
        r = j - qt
        g, p = r // (2 * pt), r % (2 * pt)
        kv_block = jnp.where(p >= pt, vb - pt, kb) + g * pt + p
        return (layer, 0, jnp.where(j < qt, qb + j, kv_block))

    n = n_q + 2 * n_part * n_groups
    return pl.pallas_call(
        functools.partial(_qkv_kernel, q_tiles=qt, part_tiles=pt, q_scale=HEAD_DIM ** -0.5),
        grid=(m // tm, n // tn),
        in_specs=[pl.BlockSpec((tm, k), lambda i, j: (i, 0)),
                  pl.BlockSpec((None, k, tn), w_block),
                  _table_spec(tm, cos.shape[0]), _table_spec(tm, sin.shape[0])],
        out_specs=pl.BlockSpec((tm, tn), lambda i, j: (i, j)),
        out_shape=jax.ShapeDtypeStruct((m, n), _F32),
        compiler_params=_params(("parallel", "arbitrary"), _mm_vmem(tm, tn, (k,), (k,), 2)),
        name=name,
    )(h, w, cos, sin)


def _kv_kernel(*refs, k_tiles, heads, stacked):
    h_ref, w_ref, cos_ref, sin_ref = refs[:4]
    o_ref = refs[-2] if stacked else refs[-1]
    st_ref = refs[-1] if stacked else None
    tm, tn = o_ref.shape
    z = _dot(h_ref[...], w_ref[...])
    j = pl.program_id(1)
    is_v = j >= k_tiles
    per_tile = tn // HEAD_DIM
    first_head = j * per_tile
    cos, sin = cos_ref[...], sin_ref[...]
    for t in range(per_tile):
        x = z[:, t * HEAD_DIM:(t + 1) * HEAD_DIM]
        x = jnp.where(is_v, x, _rotate(x, cos, sin))
        o_ref[:, t * HEAD_DIM:(t + 1) * HEAD_DIM] = x
        if stacked:
            st_ref[0, pl.ds(first_head + t, tm, stride=2 * heads), :] = x


def _mm_kv(h, w, layer, col_k, col_v, n, cos, sin, name, stacked=None):
    m, k = h.shape
    heads = n // HEAD_DIM
    tm, tn = _tile(m, 1024), _tile(n, 512 if stacked is None else 256)
    kt = n // tn
    kb, vb = col_k // tn, col_v // tn
    in_specs = [pl.BlockSpec((tm, k), lambda i, j: (i, 0)),
                pl.BlockSpec((None, k, tn), lambda i, j: (layer, 0, jnp.where(j < kt, kb + j, vb + j - kt))),
                _table_spec(tm, cos.shape[0]), _table_spec(tm, sin.shape[0])]
    args = [h, w, cos, sin]
    out_specs = [pl.BlockSpec((tm, tn), lambda i, j: (i, j))]
    out_shape = [jax.ShapeDtypeStruct((m, 2 * n), _F32)]
    aliases, extra = {}, 0
    if stacked is not None:
        rows = tm * 2 * heads
        assert stacked.shape[1:] == (m * 2 * heads, HEAD_DIM)
        in_specs.append(pl.BlockSpec(memory_space=pl.ANY))
        args.append(stacked)
        out_specs.append(pl.BlockSpec((1, rows, HEAD_DIM), lambda i, j: (layer, i, 0)))
        out_shape.append(jax.ShapeDtypeStruct(stacked.shape, _F32))
        aliases = {4: 1}
        extra = 2 * _nbytes((rows, HEAD_DIM), _F32)
    outs = pl.pallas_call(
        functools.partial(_kv_kernel, k_tiles=kt, heads=heads, stacked=stacked is not None),
        grid=(m // tm, 2 * kt),
        in_specs=in_specs,
        out_specs=out_specs,
        out_shape=out_shape,
        input_output_aliases=aliases,
        compiler_params=_params(("parallel", "arbitrary"), _mm_vmem(tm, tn, (k,), (k,), 2, extra)),
        name=name,
    )(*args)
    return outs[0] if stacked is None else tuple(outs)


def _merged_kernel(c_ref, a_ref, wc_ref, wa_ref, gc_ref, ga_ref, o_ref):
    pc = _dot(c_ref[...], wc_ref[...])
    pa = _dot(a_ref[...], wa_ref[...])
    o_ref[...] = (gc_ref[...] * pc + ga_ref[...] * pa).astype(o_ref.dtype)


def _mm_merged(c, a, w_pc, w_pa, gates, col_gc, col_ga, layer, name):
    m, kc = c.shape
    ka = a.shape[1]
    n = w_pc.shape[2]
    tm, tn = _tile(m, 1024), _tile(n, 256)
    assert col_gc % tn == 0 and col_ga % tn == 0
    gcb, gab = col_gc // tn, col_ga // tn
    out_dtype = _BF16 if m % 16 == 0 else _F32
    return pl.pallas_call(
        _merged_kernel,
        grid=(m // tm, n // tn),
        in_specs=[pl.BlockSpec((tm, kc), lambda i, j: (i, 0)),
                  pl.BlockSpec((tm, ka), lambda i, j: (i, 0)),
                  _w_spec(kc, tn, layer, 0), _w_spec(ka, tn, layer, 0),
                  pl.BlockSpec((tm, tn), lambda i, j: (i, gcb + j)),
                  pl.BlockSpec((tm, tn), lambda i, j: (i, gab + j))],
        out_specs=pl.BlockSpec((tm, tn), lambda i, j: (i, j)),
        out_shape=jax.ShapeDtypeStruct((m, n), out_dtype),
        compiler_params=_params(("parallel", "arbitrary"), _mm_vmem(tm, tn, (kc, ka), (kc, ka), 3)),
        name=name,
    )(c, a, w_pc, w_pa, gates, gates)


def _out_kernel(m_ref, w_ref, x_ref, o_ref):
    o_ref[...] = x_ref[...] + _dot(m_ref[...], w_ref[...])


def _mm_out(merged, w_o, x, layer, name):
    m, k = merged.shape
    n = w_o.shape[2]
    tm, tn = _tile(m, 1024), _tile(n, 512)
    return pl.pallas_call(
        _out_kernel,
        grid=(m // tm, n // tn),
        in_specs=[pl.BlockSpec((tm, k), lambda i, j: (i, 0)),
                  _w_spec(k, tn, layer, 0),
                  pl.BlockSpec((tm, tn), lambda i, j: (i, j))],
        out_specs=pl.BlockSpec((tm, tn), lambda i, j: (i, j)),
        out_shape=jax.ShapeDtypeStruct((m, n), _F32),
        compiler_params=_params(("parallel", "arbitrary"), _mm_vmem(tm, tn, (k,), (k,), 2)),
        name=name,
    )(merged, w_o, x)


def _ln_gate(c, g, b, gate):
    mu = jnp.mean(c, axis=-1, keepdims=True)
    xc = c - mu
    var = jnp.mean(xc * xc, axis=-1, keepdims=True)
    return _silu(xc * lax.rsqrt(var + EPS) * g + b) * gate


def _conv_prompt_kernel(u_ref, halo_ref, gate_ref, w_ref, b_ref, g_ref, be_ref, o_ref, uext, cbuf,
                        *, tiles_per_seq, rows_conv, rows_ln):
    tr, ch = u_ref.shape
    first = (pl.program_id(0) % tiles_per_seq) == 0
    lead = _HALO - (CONV_WIDTH - 1)

    for ct in range(ch // _LANES):
        cs = slice(ct * _LANES, (ct + 1) * _LANES)
        uext[ct, 0:_HALO, :] = jnp.where(first, 0.0, halo_ref[:, cs])
        uext[ct, _HALO:_HALO + tr, :] = u_ref[:, cs]

    for ct in range(ch // _LANES):
        cs = slice(ct * _LANES, (ct + 1) * _LANES)
        taps = [w_ref[j:j + 1, cs] for j in range(CONV_WIDTH)]
        bias = b_ref[:, cs]

        def conv_rows(rg, carry, ct=ct, cs=cs, taps=taps, bias=bias):
            r0 = pl.multiple_of(rg * rows_conv, rows_conv)
            acc = jnp.broadcast_to(bias, (rows_conv, _LANES))
            for j in range(CONV_WIDTH):
                acc = acc + uext[ct, pl.ds(r0 + lead + j, rows_conv), :] * taps[j]
            cbuf[pl.ds(r0, rows_conv), cs] = acc
            return carry

        lax.fori_loop(0, tr // rows_conv, conv_rows, 0)

    def ln_rows(rg, carry):
        r0 = pl.multiple_of(rg * rows_ln, rows_ln)
        y = _ln_gate(cbuf[pl.ds(r0, rows_ln), :], g_ref[...], be_ref[...], gate_ref[pl.ds(r0, rows_ln), :])
        o_ref[pl.ds(r0, rows_ln), :] = y.astype(o_ref.dtype)
        return carry

    lax.fori_loop(0, tr // rows_ln, ln_rows, 0)


def _conv_prompt(u, gates, dw_w, dw_b, ln_g, ln_b, seq, name):
    m, ch = u.shape
    tr = _tile(seq, 256)
    row = lambda v: v.reshape(1, ch)
    blk = _nbytes((tr, ch), _F32)
    return pl.pallas_call(
        functools.partial(_conv_prompt_kernel, tiles_per_seq=seq // tr, rows_conv=64, rows_ln=32),
        grid=(m // tr,),
        in_specs=[pl.BlockSpec((tr, ch), lambda i: (i, 0)),
                  pl.BlockSpec((_HALO, ch), lambda i: (jnp.maximum(i * (tr // _HALO) - 1, 0), 0)),
                  pl.BlockSpec((tr, ch), lambda i: (i, 0)),
                  pl.BlockSpec((CONV_WIDTH, ch), lambda i: (0, 0)),
                  pl.BlockSpec((1, ch), lambda i: (0, 0)),
                  pl.BlockSpec((1, ch), lambda i: (0, 0)),
                  pl.BlockSpec((1, ch), lambda i: (0, 0))],
        out_specs=pl.BlockSpec((tr, ch), lambda i: (i, 0)),
        out_shape=jax.ShapeDtypeStruct((m, ch), _BF16),
        scratch_shapes=[pltpu.VMEM((ch // _LANES, _HALO + tr, _LANES), _F32), pltpu.VMEM((tr, ch), _F32)],
        compiler_params=_params(("parallel",), 8 * blk + (10 << 20)),
        name=name,
    )(u, u, gates, dw_w, row(dw_b), row(ln_g), row(ln_b))


def _conv_sample_kernel(uext_ref, gate_ref, w_ref, b_ref, g_ref, be_ref, o_ref, cbuf):
    for b in range(uext_ref.shape[0]):
        cbuf[b:b + 1, :] = jnp.sum(uext_ref[b] * w_ref[...], axis=0, keepdims=True) + b_ref[...]
    o_ref[...] = _ln_gate(cbuf[...], g_ref[...], be_ref[...], gate_ref[...])


def _conv_sample(uext, gates, dw_w, dw_b, ln_g, ln_b, name):
    nb, _, ch = uext.shape
    row = lambda v: v.reshape(1, ch)
    whole = lambda shape: pl.BlockSpec(shape, lambda i: (0,) * len(shape))
    return pl.pallas_call(
        _conv_sample_kernel,
        grid=(1,),
        in_specs=[whole(uext.shape), whole((nb, ch)), whole(dw_w.shape), whole((1, ch)), whole((1, ch)), whole((1, ch))],
        out_specs=whole((nb, ch)),
        out_shape=jax.ShapeDtypeStruct((nb, ch), _F32),
        scratch_shapes=[pltpu.VMEM((nb, ch), _F32)],
        compiler_params=_params(("arbitrary",), 6 * _nbytes((nb, 32, ch), _F32) + (8 << 20)),
        name=name,
    )(uext, gates, dw_w, row(dw_b), row(ln_g), row(ln_b))


def _attn_prompt_kernel(q0, k0, v0, q1, k1, v1, q2, k2, v2, gate_ref, o_ref, obuf, lbuf):
    seq = o_ref.shape[0]
    ri = lax.broadcasted_iota(jnp.int32, (BLOCK, BLOCK), 0)
    ci = lax.broadcasted_iota(jnp.int32, (BLOCK, BLOCK), 1)
    keep_cur = ci <= ri
    keep_prev = ci >= ri
    nt = (((1,), (1,)), ((), ()))

    for g, ((window, dil), (q_ref, k_ref, v_ref)) in enumerate(
            zip(GROUPS, ((q0, k0, v0), (q1, k1, v1), (q2, k2, v2)))):
        assert window // dil == BLOCK and seq % (dil * BLOCK) == 0 and seq % (BLOCK * _ATTN_UNROLL) == 0
        nb = seq // (dil * BLOCK)

        def rows(start, dil=dil):
            return pl.ds(start, BLOCK) if dil == 1 else pl.ds(start, BLOCK, stride=dil)

        def blocks(it, carry, g=g, dil=dil, nb=nb, q_ref=q_ref, k_ref=k_ref, v_ref=v_ref, rows=rows):
            starts, pstarts, first, scores = [], [], [], []
            for k in range(_ATTN_UNROLL):
                blk = it * _ATTN_UNROLL + k
                res, n = blk // nb, blk % nb
                starts.append(n * (BLOCK * dil) + res)
                pstarts.append(jnp.maximum(n - 1, 0) * (BLOCK * dil) + res)
                first.append(n == 0)
            for k in range(_ATTN_UNROLL):
                q = q_ref[rows(starts[k]), :].astype(_BF16)
                kc = k_ref[rows(starts[k]), :].astype(_BF16)
                sc = jnp.where(keep_cur, lax.dot_general(q, kc, nt, preferred_element_type=_F32), NEG_INF)
                sp = None
                if nb > 1:
                    kp = k_ref[rows(pstarts[k]), :].astype(_BF16)
                    sp = lax.dot_general(q, kp, nt, preferred_element_type=_F32)
                    sp = jnp.where(jnp.logical_and(keep_prev, jnp.logical_not(first[k])), sp, NEG_INF)
                scores.append((sc, sp))
            probs = []
            for sc, sp in scores:
                if nb > 1:
                    mx = jnp.max(jnp.maximum(sc, sp), axis=-1, keepdims=True)
                    ec, ep = jnp.exp(sc - mx), jnp.exp(sp - mx)
                    den = jnp.sum(ec + ep, axis=-1, keepdims=True)
                    probs.append((ec.astype(_BF16), ep.astype(_BF16), mx, den))
                else:
                    mx = jnp.max(sc, axis=-1, keepdims=True)
                    ec = jnp.exp(sc - mx)
                    den = jnp.sum(ec, axis=-1, keepdims=True)
                    probs.append((ec.astype(_BF16), None, mx, den))
            for k, (ec, ep, mx, den) in enumerate(probs):
                o = jnp.dot(ec, v_ref[rows(starts[k]), :].astype(_BF16), preferred_element_type=_F32)
                if nb > 1:
                    o = o + jnp.dot(ep, v_ref[rows(pstarts[k]), :].astype(_BF16), preferred_element_type=_F32)
                obuf[g, rows(starts[k]), :] = o * (1.0 / den)
                lbuf[g, rows(starts[k]), :] = jnp.broadcast_to(mx + jnp.log(den), (BLOCK, HEAD_DIM))
            return carry

        lax.fori_loop(0, seq // (BLOCK * _ATTN_UNROLL), blocks, 0)

    rows_c = 256

    def combine(it, carry):
        r0 = pl.multiple_of(it * rows_c, rows_c)
        sl = pl.ds(r0, rows_c)
        l0, l1, l2 = lbuf[0, sl, :], lbuf[1, sl, :], lbuf[2, sl, :]
        mx = jnp.maximum(jnp.maximum(l0, l1), l2)
        e0, e1, e2 = jnp.exp(l0 - mx), jnp.exp(l1 - mx), jnp.exp(l2 - mx)
        inv = 1.0 / (e0 + e1 + e2)
        o = (e0 * inv) * obuf[0, sl, :] + (e1 * inv) * obuf[1, sl, :] + (e2 * inv) * obuf[2, sl, :]
        o_ref[sl, :] = (o * gate_ref[sl, :]).astype(o_ref.dtype)
        return carry

    lax.fori_loop(0, seq // rows_c, combine, 0)


def _attn_prompt(qkv, kv_last, gates, col_gate, batch, seq, hg, name):
    ao = hg * HEAD_DIM
    qkv3 = qkv.reshape(batch, seq, qkv.shape[1])
    last3 = kv_last.reshape(batch, seq, kv_last.shape[1])
    blk = (None, seq, HEAD_DIM)
    in_specs, args = [], []
    for g in range(N_GROUPS):
        in_specs.append(pl.BlockSpec(blk, lambda b, h, g=g: (b, 0, g * hg + h)))
        args.append(qkv3)
        if g < N_GROUPS - 1:
            k_block = (N_GROUPS + 2 * g) * hg
            in_specs += [pl.BlockSpec(blk, lambda b, h, kb=k_block: (b, 0, kb + h)),
                         pl.BlockSpec(blk, lambda b, h, vb=k_block + hg: (b, 0, vb + h))]
            args += [qkv3, qkv3]
        else:
            in_specs += [pl.BlockSpec(blk, lambda b, h: (b, 0, h)),
                         pl.BlockSpec(blk, lambda b, h: (b, 0, hg + h))]
            args += [last3, last3]
    gate_block = col_gate // HEAD_DIM
    in_specs.append(pl.BlockSpec(blk, lambda b, h: (b, 0, gate_block + h)))
    args.append(gates.reshape(batch, seq, gates.shape[1]))
    tile = _nbytes((seq, HEAD_DIM), _F32)
    out = pl.pallas_call(
        _attn_prompt_kernel,
        grid=(batch, hg),
        in_specs=in_specs,
        out_specs=pl.BlockSpec(blk, lambda b, h: (b, 0, h)),
        out_shape=jax.ShapeDtypeStruct((batch, seq, ao), _BF16),
        scratch_shapes=[pltpu.VMEM((N_GROUPS, seq, HEAD_DIM), _F32),
                        pltpu.VMEM((N_GROUPS, seq, HEAD_DIM), _F32)],
        compiler_params=_params(("parallel", "parallel"), 28 * tile + (8 << 20)),
        name=name,
    )(*args)
    return out.reshape(batch * seq, ao)


def _attn_sample_kernel(q_ref, kv0, kv1, kv2, kb0, vb0, kb1, vb1, kb2, vb2, gate_ref, o_ref):
    outs, lses = [], []
    for g, (new_ref, kb_ref, vb_ref) in enumerate(((kv0, kb0, vb0), (kv1, kb1, vb1), (kv2, kb2, vb2))):
        q = q_ref[g]
        k_new, v_new = new_ref[0], new_ref[1]
        s_buf = jnp.sum(kb_ref[...] * q[None], axis=-1, keepdims=True)
        s_new = jnp.sum(k_new * q, axis=-1, keepdims=True)
        mx = jnp.maximum(jnp.max(s_buf, axis=0), s_new)
        e_buf = jnp.exp(s_buf - mx[None])
        e_new = jnp.exp(s_new - mx)
        den = jnp.sum(e_buf, axis=0) + e_new
        inv = 1.0 / den
        outs.append(jnp.sum((e_buf * inv[None]) * vb_ref[...], axis=0) + (e_new * inv) * v_new)
        lses.append(mx + jnp.log(den))
    mx = jnp.maximum(jnp.maximum(lses[0], lses[1]), lses[2])
    es = [jnp.exp(l - mx) for l in lses]
    inv = 1.0 / (es[0] + es[1] + es[2])
    o = (es[0] * inv) * outs[0] + (es[1] * inv) * outs[1] + (es[2] * inv) * outs[2]
    o_ref[...] = o * gate_ref[...]


def _attn_sample(q, kv_new, caches, gate, layer, name):
    nb, _, hg, _ = q.shape
    in_specs = [pl.BlockSpec((None, N_GROUPS, hg, HEAD_DIM), lambda b: (b, 0, 0, 0))]
    args = [q]
    for g in range(N_GROUPS):
        in_specs.append(pl.BlockSpec((None, 2, hg, HEAD_DIM), lambda b: (b, 0, 0, 0)))
        args.append(kv_new[g])
    for g, (window, dil) in enumerate(GROUPS):
        cache = caches[g]
        assert cache.shape[2] == window, "window buffer shorter than the window is not supported"
        view = cache.reshape(cache.shape[0], nb, window // dil, dil, 2, hg, HEAD_DIM)
        for kv in range(2):
            in_specs.append(pl.BlockSpec((None, None, window // dil, None, None, hg, HEAD_DIM),
                                         lambda b, kv=kv: (layer, b, 0, 0, kv, 0, 0)))
            args.append(view)
    in_specs.append(pl.BlockSpec((None, hg, HEAD_DIM), lambda b: (b, 0, 0)))
    args.append(gate)
    out = pl.pallas_call(
        _attn_sample_kernel,
        grid=(nb,),
        in_specs=in_specs,
        out_specs=pl.BlockSpec((None, hg, HEAD_DIM), lambda b: (b, 0, 0)),
        out_shape=jax.ShapeDtypeStruct((nb, hg, HEAD_DIM), _F32),
        compiler_params=_params(("parallel",), 24 * _nbytes((BLOCK, hg, HEAD_DIM), _F32) + (8 << 20)),
        name=name,
    )(*args)
    return out.reshape(nb, hg * HEAD_DIM)


def _cache_update_kernel(cur_ref, nxt_ref, new_ref, o_ref, *, last):
    rows, step = cur_ref.shape[0], nxt_ref.shape[0]
    o_ref[0:rows - step, :] = cur_ref[step:rows, :]
    o_ref[rows - step:rows, :] = jnp.where(pl.program_id(1) == last, new_ref[...], nxt_ref[...])


def _cache_update(cache, new, name):
    depth, nb, window, two, hg, hd = cache.shape
    step = two * hg
    slabs, total = depth * nb, window * step
    rows = _tile(total, 8192)
    nblk = total // rows
    view = cache.reshape(slabs, total, hd)
    out = pl.pallas_call(
        functools.partial(_cache_update_kernel, last=nblk - 1),
        grid=(slabs, nblk),
        in_specs=[pl.BlockSpec((None, rows, hd), lambda s, j: (s, j, 0)),
                  pl.BlockSpec((None, step, hd),
                               lambda s, j: (s, jnp.minimum((j + 1) * (rows // step), window - 1), 0)),
                  pl.BlockSpec((None, step, hd), lambda s, j: (s, 0, 0))],
        out_specs=pl.BlockSpec((None, rows, hd), lambda s, j: (s, j, 0)),
        out_shape=jax.ShapeDtypeStruct(view.shape, cache.dtype),
        compiler_params=_params(("parallel", "arbitrary"), 6 * _nbytes((rows, hd), _F32) + (4 << 20)),
        name=name,
    )(view, view, new.reshape(slabs, step, hd))
    return out.reshape(cache.shape)


def _rope_tables(pos, rows):
    half = HEAD_DIM // 2
    inv_freq = ROPE_THETA ** (-jnp.arange(half, dtype=_F32) / half)
    ang = pos.astype(_F32)[:, None] * inv_freq[None, :]
    cos, sin = jnp.cos(ang), jnp.sin(ang)
    cos = jnp.broadcast_to(jnp.concatenate([cos, cos], axis=-1), (rows, HEAD_DIM))
    sin = jnp.broadcast_to(jnp.concatenate([-sin, sin], axis=-1), (rows, HEAD_DIM))
    return cos, sin


def _projections(x, tables, layer, norm_g, w_in, tag, stacked=None):
    m, d = x.shape
    hg = d // (4 * HEAD_DIM)
    qkv, ao = N_GROUPS * hg * HEAD_DIM, hg * HEAD_DIM
    col = {}
    off = 0
    for nm, size in (("a", d), ("b", d), ("cg", d), ("q", qkv), ("k", qkv), ("v", qkv), ("ag", ao), ("g", 2 * d)):
        col[nm] = off
        off += size
    assert off == w_in.shape[2]
    cos, sin = tables
    h = _rmsnorm(x, norm_g[layer], _BF16 if m % 16 == 0 else _F32, f"rms_{tag}")
    u = _mm_glu(h, w_in, layer, col["a"], col["b"], d, f"glu_{tag}")
    gates = _mm_gates(h, w_in, layer, col["cg"], d, col["ag"], ao, 2 * d, f"gates_{tag}")
    qk = _mm_qkv(h, w_in, layer, col["q"], qkv, col["k"], col["v"], ao, N_GROUPS - 1, cos, sin, f"qkv_{tag}")
    last = N_GROUPS - 1
    kv_last = _mm_kv(h, w_in, layer, col["k"] + last * ao, col["v"] + last * ao, ao, cos, sin, f"kv{last}_{tag}",
                     stacked=stacked)
    return u, gates, qk, kv_last


def kernel(x_prompt, x_sample, cache_kv_w128, cache_kv_w512, cache_kv_w2048, state_conv, norm_g, w_in,
           dw_w, dw_b, ln_g, ln_b, w_pc, w_pa, w_o, final_g):
    batch, seq, d = x_prompt.shape
    nb_s, seq_s, _ = x_sample.shape
    assert seq_s == 1, "the sample group is one new position per row"
    depth = w_in.shape[0]
    hg = d // (4 * HEAD_DIM)
    ao = hg * HEAD_DIM
    caches = (cache_kv_w128, cache_kv_w512, cache_kv_w2048)
    col_ag, col_gc, col_ga = d, d + ao, 2 * d + ao
    kv_col = lambda g: (N_GROUPS + 2 * g) * ao

    tab_p = _rope_tables(jnp.arange(seq, dtype=jnp.int32), seq)
    tab_s = _rope_tables(jnp.full((1,), PAST_LEN, jnp.int32), nb_s)

    xs = x_sample.reshape(nb_s, d)
    conv_s = []
    new_rows = [[] for _ in GROUPS]
    for l in range(depth):
        u, gates, qk, kv_last = _projections(xs, tab_s, l, norm_g, w_in, f"s{l}")
        uext = jnp.concatenate([state_conv[l], u[:, None, :]], axis=1)
        c = _conv_sample(uext, gates[:, :d], dw_w[l], dw_b[l], ln_g[l], ln_b[l], f"conv_s{l}")
        kv_new = [qk[:, kv_col(g):kv_col(g) + 2 * ao].reshape(nb_s, 2, hg, HEAD_DIM) for g in range(N_GROUPS - 1)]
        kv_new.append(kv_last.reshape(nb_s, 2, hg, HEAD_DIM))
        a = _attn_sample(qk[:, :N_GROUPS * ao].reshape(nb_s, N_GROUPS, hg, HEAD_DIM), kv_new, caches,
                         gates[:, col_ag:col_gc].reshape(nb_s, hg, HEAD_DIM), l, f"attn_s{l}")
        merged = _mm_merged(c, a, w_pc, w_pa, gates, col_gc, col_ga, l, f"merged_s{l}")
        xs = _mm_out(merged, w_o, xs, l, f"out_s{l}")
        conv_s.append(uext[:, 1:])
        for g in range(N_GROUPS):
            new_rows[g].append(kv_new[g])
    kv_s = [_cache_update(caches[g], jnp.stack(new_rows[g]), f"cache_update{g}") for g in range(N_GROUPS)]

    m = batch * seq
    xp = x_prompt.reshape(m, d)
    conv_p = []
    kv_p = [[] for _ in GROUPS]
    stack_last = GROUPS[-1][0] >= seq
    stacked = jnp.zeros((depth, m * 2 * hg, HEAD_DIM), _F32) if stack_last else None
    for l in range(depth):
        u, gates, qk, kv_last = _projections(xp, tab_p, l, norm_g, w_in, f"p{l}", stacked=stacked)
        if stack_last:
            kv_last, stacked = kv_last
        c = _conv_prompt(u, gates, dw_w[l], dw_b[l], ln_g[l], ln_b[l], seq, f"conv_p{l}")
        a = _attn_prompt(qk, kv_last, gates, col_ag, batch, seq, hg, f"attn_p{l}")
        merged = _mm_merged(c, a, w_pc, w_pa, gates, col_gc, col_ga, l, f"merged_p{l}")
        xp = _mm_out(merged, w_o, xp, l, f"out_p{l}")
        conv_p.append(u.reshape(batch, seq, d)[:, seq - (CONV_WIDTH - 1):])
        for g, (window, _) in enumerate(GROUPS):
            if g == N_GROUPS - 1 and stack_last:
                continue
            kv = kv_last if g == N_GROUPS - 1 else qk[:, kv_col(g):kv_col(g) + 2 * ao]
            kv_p[g].append(kv.reshape(batch, seq, 2, hg, HEAD_DIM)[:, seq - min(window, seq):])
    kv_p = [jnp.stack(rows) if rows else None for rows in kv_p]
    if stack_last:
        kv_p[-1] = stacked.reshape(depth, batch, seq, 2, hg, HEAD_DIM)

    y_prompt = _rmsnorm(xp, final_g, _F32, "final_p").reshape(batch, seq, d)
    y_sample = _rmsnorm(xs, final_g, _F32, "final_s").reshape(nb_s, 1, d)
    return (y_prompt, y_sample, kv_p[0], kv_p[1], kv_p[2], jnp.stack(conv_p),
            kv_s[0], kv_s[1], kv_s[2], jnp.stack(conv_s))
```
